```python
import math
import jax, jax.numpy as jnp
from jax import lax
import numpy as np

D_MODEL = 1024
BATCH = 1
SEQ = 16384
DEPTH = 4

N_MIXERS = 3
HGRN_EXPAND = 128
HGRN_HEADS = D_MODEL // HGRN_EXPAND
HGRN_DV = D_MODEL // HGRN_HEADS
HGRN_CHUNK = 64
ATTN_HEADS = 8
HEAD_DIM = D_MODEL // ATTN_HEADS
MOBA_BLOCK = 256
MOBA_TOPK = 3
MOBA_QCHUNK = 64
ROPE_THETA = 500000.0
ROPE_DIM = HEAD_DIM // 4
CONV_WIDTH = 3
D_FF = 2816
FFN_CONV_WIDTH = 3
LN_EPS = 1e-5
RMS_EPS = 1e-6
DEEPNORM_ALPHA = (2.0 * DEPTH) ** 0.25
DEEPNORM_BETA = (8.0 * DEPTH) ** -0.25
MASK_NEG = -1e30

kernel_name = 'hybrid_hgrn2_moba_shortconv_deepnorm'


def layer_norm(x, g, b):
    xf = x.astype(jnp.float32)
    mu = jnp.mean(xf, axis=-1, keepdims=True)
    var = jnp.mean(jnp.square(xf - mu), axis=-1, keepdims=True)
    y = (xf - mu) * lax.rsqrt(var + LN_EPS) * g.astype(jnp.float32) + b.astype(jnp.float32)
    return y.astype(x.dtype)


def causal_dwconv(x, w):
    W = w.shape[0]
    S = x.shape[1]
    xp = jnp.pad(x, ((0, 0), (W - 1, 0), (0, 0)))
    return sum(xp[:, j:j + S, :] * w[j] for j in range(W))


def partial_rope(x, positions):
    half = ROPE_DIM // 2
    inv_freq = 1.0 / (ROPE_THETA ** (jnp.arange(0, ROPE_DIM, 2, dtype=jnp.float32) / ROPE_DIM))
    ang = positions.astype(jnp.float32)[..., None] * inv_freq
    cos = jnp.cos(ang)[:, :, None, :]
    sin = jnp.sin(ang)[:, :, None, :]
    xf = x.astype(jnp.float32)
    x1 = xf[..., :half]
    x2 = xf[..., half:ROPE_DIM]
    out = jnp.concatenate([x1 * cos - x2 * sin, x2 * cos + x1 * sin, xf[..., ROPE_DIM:]], axis=-1)
    return out.astype(x.dtype)


def hgrn_lower_bound(lower_bounds, layer):
    s = jax.nn.softmax(lower_bounds.astype(jnp.float32), axis=0)
    lb = jnp.cumsum(s, axis=0) - s[0]
    return lb[layer]


def hgrn2_mixer(x, lb, w_in, norm_g, w_out):
    B, S, _ = x.shape
    H, dk, dv, C = HGRN_HEADS, HGRN_EXPAND, HGRN_DV, HGRN_CHUNK
    NC = S // C
    proj = x @ w_in
    q, fz, i, gate = jnp.split(proj, [H * dk, 2 * H * dk, 2 * H * dk + H * dv], axis=-1)
    q = jax.nn.silu(q.astype(jnp.float32)).reshape(B, S, H, dk)
    fz = fz.astype(jnp.float32).reshape(B, S, H, dk)
    lb = lb.reshape(H, dk)
    sig = jax.nn.sigmoid(fz)
    k = (1.0 - lb) * (1.0 - sig)
    g = jnp.log(lb + (1.0 - lb) * sig)
    v = i.astype(jnp.float32).reshape(B, S, H, dv)

    def to_chunks(t):
        return t.reshape(B, NC, C, H, t.shape[-1]).transpose(1, 0, 3, 2, 4)

    qc, kc, vc, gc = to_chunks(q), to_chunks(k), to_chunks(v), to_chunks(g)
    bc = jnp.cumsum(gc, axis=3)
    causal = jnp.tril(jnp.ones((C, C), dtype=bool))[:, :, None]

    def step(state, inp):
        q_, k_, v_, b_ = inp
        o_inter = jnp.einsum('bhtd,bhdv->bhtv', q_ * jnp.exp(b_), state)
        diff = b_[:, :, :, None, :] - b_[:, :, None, :, :]
        decay = jnp.where(causal, jnp.exp(jnp.where(causal, diff, 0.0)), 0.0)
        a = jnp.einsum('bhtd,bhsd,bhtsd->bhts', q_, k_, decay)
        o = o_inter + jnp.einsum('bhts,bhsv->bhtv', a, v_)
        b_last = b_[:, :, -1, :]
        k_dec = k_ * jnp.exp(b_last[:, :, None, :] - b_)
        state = jnp.exp(b_last)[..., None] * state + jnp.einsum('bhsd,bhsv->bhdv', k_dec, v_)
        return state, o

    state0 = jnp.zeros((B, H, dk, dv), jnp.float32)
    _, oc = lax.scan(step, state0, (qc, kc, vc, bc))
    o = oc.transpose(1, 0, 3, 2, 4).reshape(B, S, H, dv)
    o = o * lax.rsqrt(jnp.mean(jnp.square(o), axis=-1, keepdims=True) + RMS_EPS)
    o = o.reshape(B, S, H * dv) * norm_g.astype(jnp.float32) * jax.nn.silu(gate.astype(jnp.float32))
    return o.astype(x.dtype) @ w_out


def moba_mixer(x, positions, w_in, w_out):
    B, S, _ = x.shape
    H, hd, L = ATTN_HEADS, HEAD_DIM, MOBA_BLOCK
    qkv = (x @ w_in).reshape(B, S, 3, H, hd)
    q = partial_rope(qkv[:, :, 0], positions)
    k = partial_rope(qkv[:, :, 1], positions)
    v = qkv[:, :, 2]
    Sp = -(-S // L) * L
    pad = Sp - S
    q, k, v = [jnp.pad(t, ((0, 0), (0, pad), (0, 0), (0, 0))) for t in (q, k, v)]
    NB = Sp // L
    K = min(MOBA_TOPK, NB)
    kb = k.reshape(B, NB, L, H, hd).transpose(0, 1, 3, 2, 4)
    vb = v.reshape(B, NB, L, H, hd).transpose(0, 1, 3, 2, 4)
    kmean = jnp.mean(kb.astype(jnp.float32), axis=3)
    gate = jnp.einsum('bshd,bnhd->bshn', q.astype(jnp.float32), kmean)
    qblk = jnp.arange(Sp) // L
    past = jnp.arange(NB)[None, :] < qblk[:, None]
    gate = jnp.where(past[None, :, None, :], gate, MASK_NEG)
    _, sel = lax.top_k(gate, K)

    QC = MOBA_QCHUNK
    NQ = Sp // QC
    q_chunks = q.reshape(B, NQ, QC, H, hd).transpose(1, 0, 2, 3, 4)
    sel_chunks = sel.reshape(B, NQ, QC, H, K).transpose(1, 0, 2, 3, 4)
    bidx = jnp.arange(B)[:, None, None, None]
    hidx = jnp.arange(H)[None, None, :, None]
    scale = 1.0 / math.sqrt(hd)

    def attend(args):
        qc, sc, ci = args
        start = ci * QC
        blk = start // L
        k_sel = kb[bidx, sc, hidx]
        v_sel = vb[bidx, sc, hidx]
        k_own = lax.dynamic_index_in_dim(kb, blk, axis=1, keepdims=False)
        v_own = lax.dynamic_index_in_dim(vb, blk, axis=1, keepdims=False)
        s_sel = jnp.einsum('bqhd,bqhjld->bqhjl', qc, k_sel).astype(jnp.float32) * scale
        slot_ok = jnp.arange(K) < blk
        s_sel = jnp.where(slot_ok[:, None], s_sel, MASK_NEG).reshape(B, QC, H, K * L)
        s_own = jnp.einsum('bqhd,bhld->bqhl', qc, k_own).astype(jnp.float32) * scale
        qpos = start + jnp.arange(QC)
        kpos = blk * L + jnp.arange(L)
        s_own = jnp.where((kpos[None, :] <= qpos[:, None])[None, :, None, :], s_own, MASK_NEG)
        p = jax.nn.softmax(jnp.concatenate([s_sel, s_own], axis=-1), axis=-1).astype(v.dtype)
        p_sel = p[..., :K * L].reshape(B, QC, H, K, L)
        p_own = p[..., K * L:]
        return (jnp.einsum('bqhjl,bqhjld->bqhd', p_sel, v_sel)
                + jnp.einsum('bqhl,bhld->bqhd', p_own, v_own))

    o = lax.map(attend, (q_chunks, sel_chunks, jnp.arange(NQ)))
    o = o.transpose(1, 0, 2, 3, 4).reshape(B, Sp, H * hd)[:, :S]
    return o @ w_out


def short_conv_mixer(x, w_in, conv_w, w_out):
    bg, cg, h = jnp.split(x @ w_in, 3, axis=-1)
    y = bg * causal_dwconv(cg * h, conv_w)
    return y @ w_out


def conv_ffn(x, w_up, conv_w, w_down):
    u = causal_dwconv(x @ w_up, conv_w)
    a, b = jnp.split(u, 2, axis=-1)
    return (jax.nn.silu(a) * b) @ w_down


def setup_inputs(seed: int = 0) -> dict:
    key = jax.random.key(seed)
    keys = iter(jax.random.split(key, 64))

    def nrm(shape, scale):
        return jax.random.normal(next(keys), shape, jnp.float32) * scale

    D = D_MODEL
    p = {}
    p['x'] = nrm((BATCH, SEQ, D), 1.0)
    p['positions'] = jnp.tile(jnp.arange(SEQ, dtype=jnp.int32)[None, :], (BATCH, 1))
    p['hgrn_lower_bounds'] = nrm((DEPTH, HGRN_HEADS * HGRN_EXPAND), 0.5)
    for i in range(DEPTH):
        kind = i % N_MIXERS
        if kind == 0:
            p[f'l{i}_mix_w_in'] = nrm((D, 2 * HGRN_HEADS * HGRN_EXPAND + 2 * HGRN_HEADS * HGRN_DV), D ** -0.5)
            p[f'l{i}_mix_norm_g'] = 1.0 + nrm((HGRN_HEADS * HGRN_DV,), 0.02)
            p[f'l{i}_mix_w_out'] = nrm((HGRN_HEADS * HGRN_DV, D), DEEPNORM_BETA * (HGRN_HEADS * HGRN_DV) ** -0.5)
        elif kind == 1:
            p[f'l{i}_mix_w_in'] = nrm((D, 3 * ATTN_HEADS * HEAD_DIM), D ** -0.5)
            p[f'l{i}_mix_w_out'] = nrm((ATTN_HEADS * HEAD_DIM, D), DEEPNORM_BETA * (ATTN_HEADS * HEAD_DIM) ** -0.5)
        else:
            p[f'l{i}_mix_w_in'] = nrm((D, 3 * D), D ** -0.5)
            p[f'l{i}_mix_conv'] = nrm((CONV_WIDTH, D), CONV_WIDTH ** -0.5)
            p[f'l{i}_mix_w_out'] = nrm((D, D), DEEPNORM_BETA * D ** -0.5)
        p[f'l{i}_ln1_g'] = 1.0 + nrm((D,), 0.02)
        p[f'l{i}_ln1_b'] = nrm((D,), 0.02)
        p[f'l{i}_ffn_w_up'] = nrm((D, 2 * D_FF), D ** -0.5)
        p[f'l{i}_ffn_conv'] = nrm((FFN_CONV_WIDTH, 2 * D_FF), FFN_CONV_WIDTH ** -0.5)
        p[f'l{i}_ffn_w_down'] = nrm((D_FF, D), DEEPNORM_BETA * D_FF ** -0.5)
        p[f'l{i}_ln2_g'] = 1.0 + nrm((D,), 0.02)
        p[f'l{i}_ln2_b'] = nrm((D,), 0.02)
    return p


def reference(x, positions, hgrn_lower_bounds,
              l0_mix_w_in, l0_mix_norm_g, l0_mix_w_out, l0_ln1_g, l0_ln1_b,
              l0_ffn_w_up, l0_ffn_conv, l0_ffn_w_down, l0_ln2_g, l0_ln2_b,
              l1_mix_w_in, l1_mix_w_out, l1_ln1_g, l1_ln1_b,
              l1_ffn_w_up, l1_ffn_conv, l1_ffn_w_down, l1_ln2_g, l1_ln2_b,
              l2_mix_w_in, l2_mix_conv, l2_mix_w_out, l2_ln1_g, l2_ln1_b,
              l2_ffn_w_up, l2_ffn_conv, l2_ffn_w_down, l2_ln2_g, l2_ln2_b,
              l3_mix_w_in, l3_mix_norm_g, l3_mix_w_out, l3_ln1_g, l3_ln1_b,
              l3_ffn_w_up, l3_ffn_conv, l3_ffn_w_down, l3_ln2_g, l3_ln2_b):
    layers = [
        ((l0_mix_w_in, l0_mix_norm_g, l0_mix_w_out), l0_ln1_g, l0_ln1_b,
         (l0_ffn_w_up, l0_ffn_conv, l0_ffn_w_down), l0_ln2_g, l0_ln2_b),
        ((l1_mix_w_in, l1_mix_w_out), l1_ln1_g, l1_ln1_b,
         (l1_ffn_w_up, l1_ffn_conv, l1_ffn_w_down), l1_ln2_g, l1_ln2_b),
        ((l2_mix_w_in, l2_mix_conv, l2_mix_w_out), l2_ln1_g, l2_ln1_b,
         (l2_ffn_w_up, l2_ffn_conv, l2_ffn_w_down), l2_ln2_g, l2_ln2_b),
        ((l3_mix_w_in, l3_mix_norm_g, l3_mix_w_out), l3_ln1_g, l3_ln1_b,
         (l3_ffn_w_up, l3_ffn_conv, l3_ffn_w_down), l3_ln2_g, l3_ln2_b),
    ]
    for i in range(DEPTH):
        mix_p, ln1_g, ln1_b, ffn_p, ln2_g, ln2_b = layers[i]
        kind = i % N_MIXERS
        if kind == 0:
            m = hgrn2_mixer(x, hgrn_lower_bound(hgrn_lower_bounds, i), *mix_p)
        elif kind == 1:
            m = moba_mixer(x, positions, *mix_p)
        else:
            m = short_conv_mixer(x, *mix_p)
        x = layer_norm(DEEPNORM_ALPHA * x + m, ln1_g, ln1_b)
        x = layer_norm(DEEPNORM_ALPHA * x + conv_ffn(x, *ffn_p), ln2_g, ln2_b)
    return x
```

```python
import functools
import math

import jax
import jax.numpy as jnp
from jax import lax
from jax.experimental import pallas as pl
from jax.experimental.pallas import tpu as pltpu

F32 = jnp.float32
BF16 = jnp.bfloat16

DEPTH = 4
N_MIXERS = 3
HGRN_HEADS = 8
HGRN_DK = 128
ATTN_HEADS = 8
HEAD_DIM = 128
MOBA_BLOCK = 256
MOBA_TOPK = 3
ROPE_THETA = 500000.0
ROPE_DIM = HEAD_DIM // 4
ROPE_HALF = ROPE_DIM // 2
LN_EPS = 1e-5
RMS_EPS = 1e-6
DEEPNORM_ALPHA = (2.0 * DEPTH) ** 0.25
MASK_NEG = -1e30

V7X_VMEM_BYTES = 64 * 1024 * 1024
VMEM_LIMIT_BYTES = V7X_VMEM_BYTES - 8 * 1024 * 1024
SUBLANES = 8
LANES = 128
MXU_DIM = 256

ROW_TILE = 512
HGRN_CHUNK = 64
HGRN_SUB = SUBLANES
Q_SCALE = (1.0 / math.sqrt(HEAD_DIM)) * math.log2(math.e)


def _params(*sem):
    return pltpu.CompilerParams(dimension_semantics=sem, vmem_limit_bytes=VMEM_LIMIT_BYTES)


def _const_spec(shape):
    nd = len(shape)
    return pl.BlockSpec(shape, lambda *_: (0,) * nd, pipeline_mode=pl.Buffered(1))


def _sigmoid(x):
    return 1.0 / (1.0 + jnp.exp(-x))


def _layer_norm(y, g, b):
    mu = jnp.mean(y, axis=-1, keepdims=True)
    yc = y - mu
    var = jnp.mean(yc * yc, axis=-1, keepdims=True)
    return yc * lax.rsqrt(var + LN_EPS) * g + b


def _dot(a, b):
    return jnp.dot(a, b, preferred_element_type=F32)


def _dot_nt(a, b):
    return lax.dot_general(a, b, (((1,), (1,)), ((), ())), preferred_element_type=F32)


def _causal_conv3(u, w, carry_ref, col0):
    width = u.shape[1]
    cols = slice(col0, col0 + width)
    row = lax.broadcasted_iota(jnp.int32, (SUBLANES, width), 0)
    y = u * w[2:3]
    for j in (1, 2):
        r = pltpu.roll(u, j, 0)
        prev = carry_ref[j - 1, :, cols]
        carry_ref[j - 1, :, cols] = r[:SUBLANES]
        head = jnp.where(row < j, prev, r[:SUBLANES])
        y = y + jnp.concatenate([head, r[SUBLANES:]], axis=0) * w[2 - j:3 - j]
    return y


def _proj_ln_kernel(a_ref, w_ref, x_ref, g_ref, b_ref, of_ref, ob_ref):
    m = _dot(a_ref[...], w_ref[...])
    y = _layer_norm(DEEPNORM_ALPHA * x_ref[...] + m, g_ref[...], b_ref[...])
    of_ref[...] = y
    ob_ref[...] = y.astype(BF16)


def _proj_ln(a, w, x, g, b):
    S, D = x.shape
    K = a.shape[1]
    T = min(ROW_TILE, S)
    return pl.pallas_call(
        _proj_ln_kernel,
        grid=(S // T,),
        in_specs=[
            pl.BlockSpec((T, K), lambda i: (i, 0)),
            _const_spec((K, D)),
            pl.BlockSpec((T, D), lambda i: (i, 0)),
            _const_spec((1, D)),
            _const_spec((1, D)),
        ],
        out_specs=[pl.BlockSpec((T, D), lambda i: (i, 0)), pl.BlockSpec((T, D), lambda i: (i, 0))],
        out_shape=[jax.ShapeDtypeStruct((S, D), F32), jax.ShapeDtypeStruct((S, D), BF16)],
        compiler_params=_params("arbitrary"),
        name="proj_ln",
    )(a, w.astype(BF16), x, g.reshape(1, D), b.reshape(1, D))


def _ffn_kernel(xb_ref, xf_ref, wup_ref, cw_ref, wdn_ref, g_ref, b_ref, of_ref, ob_ref,
                carry_ref, h_ref, *, d_ff, tile):
    @pl.when(pl.program_id(0) == 0)
    def _():
        carry_ref[...] = jnp.zeros_like(carry_ref)

    xb = xb_ref[...]
    for c in range(d_ff // tile):
        ys = []
        for part in range(2):
            col0 = part * d_ff + c * tile
            u = _dot(xb, wup_ref[:, col0:col0 + tile])
            ys.append(_causal_conv3(u, cw_ref[:, col0:col0 + tile], carry_ref, col0))
        ya, yb = ys
        h_ref[:, c * tile:(c + 1) * tile] = (ya * _sigmoid(ya) * yb).astype(BF16)
    m = _dot(h_ref[...], wdn_ref[...])
    y = _layer_norm(DEEPNORM_ALPHA * xf_ref[...] + m, g_ref[...], b_ref[...])
    of_ref[...] = y
    ob_ref[...] = y.astype(BF16)


def _ffn(xb, xf, w_up, conv_w, w_down, g, b):
    S, D = xf.shape
    d_ff = w_down.shape[0]
    T = min(ROW_TILE, S)
    tile = MXU_DIM
    assert d_ff % tile == 0
    return pl.pallas_call(
        functools.partial(_ffn_kernel, d_ff=d_ff, tile=tile),
        grid=(S // T,),
        in_specs=[
            pl.BlockSpec((T, D), lambda i: (i, 0)),
            pl.BlockSpec((T, D), lambda i: (i, 0)),
            _const_spec((D, 2 * d_ff)),
            _const_spec((3, 2 * d_ff)),
            _const_spec((d_ff, D)),
            _const_spec((1, D)),
            _const_spec((1, D)),
        ],
        out_specs=[pl.BlockSpec((T, D), lambda i: (i, 0)), pl.BlockSpec((T, D), lambda i: (i, 0))],
        out_shape=[jax.ShapeDtypeStruct((S, D), F32), jax.ShapeDtypeStruct((S, D), BF16)],
        scratch_shapes=[
            pltpu.VMEM((2, SUBLANES, 2 * d_ff), F32),
            pltpu.VMEM((T, d_ff), BF16),
        ],
        compiler_params=_params("arbitrary"),
        name="conv_ffn",
    )(xb, xf, w_up.astype(BF16), conv_w, w_down.astype(BF16), g.reshape(1, D), b.reshape(1, D))


def _conv_mixer_kernel(xb_ref, xf_ref, win_ref, cw_ref, wout_ref, g_ref, b_ref, of_ref, ob_ref,
                       carry_ref, y_ref, *, d_model, tile):
    @pl.when(pl.program_id(0) == 0)
    def _():
        carry_ref[...] = jnp.zeros_like(carry_ref)

    xb = xb_ref[...]
    for c in range(d_model // tile):
        col0 = c * tile
        bg = _dot(xb, win_ref[:, col0:col0 + tile])
        cg = _dot(xb, win_ref[:, d_model + col0:d_model + col0 + tile])
        hh = _dot(xb, win_ref[:, 2 * d_model + col0:2 * d_model + col0 + tile])
        conv = _causal_conv3(cg * hh, cw_ref[:, col0:col0 + tile], carry_ref, col0)
        y_ref[:, col0:col0 + tile] = (bg * conv).astype(BF16)
    m = _dot(y_ref[...], wout_ref[...])
    y = _layer_norm(DEEPNORM_ALPHA * xf_ref[...] + m, g_ref[...], b_ref[...])
    of_ref[...] = y
    ob_ref[...] = y.astype(BF16)


def _conv_mixer(xb, xf, w_in, conv_w, w_out, g, b):
    S, D = xf.shape
    T = min(ROW_TILE, S)
    return pl.pallas_call(
        functools.partial(_conv_mixer_kernel, d_model=D, tile=MXU_DIM),
        grid=(S // T,),
        in_specs=[
            pl.BlockSpec((T, D), lambda i: (i, 0)),
            pl.BlockSpec((T, D), lambda i: (i, 0)),
            _const_spec((D, 3 * D)),
            _const_spec((3, D)),
            _const_spec((D, D)),
            _const_spec((1, D)),
            _const_spec((1, D)),
        ],
        out_specs=[pl.BlockSpec((T, D), lambda i: (i, 0)), pl.BlockSpec((T, D), lambda i: (i, 0))],
        out_shape=[jax.ShapeDtypeStruct((S, D), F32), jax.ShapeDtypeStruct((S, D), BF16)],
        scratch_shapes=[
            pltpu.VMEM((2, SUBLANES, D), F32),
            pltpu.VMEM((T, D), BF16),
        ],
        compiler_params=_params("arbitrary"),
        name="conv_mixer",
    )(xb, xf, w_in.astype(BF16), conv_w, w_out.astype(BF16), g.reshape(1, D), b.reshape(1, D))


def _hgrn_kernel(xb_ref, w_ref, lb_ref, ng_ref, o_ref, st_ref, *, rows):
    C, SUB, DK = HGRN_CHUNK, HGRN_SUB, HGRN_DK
    nch = rows // C
    nsub = C // SUB
    width = nch * DK

    @pl.when(pl.program_id(1) == 0)
    def _():
        st_ref[...] = jnp.zeros_like(st_ref)

    proj = _dot(xb_ref[...], w_ref[...])
    qz, fz, v, gz = (proj[:, j * DK:(j + 1) * DK] for j in range(4))
    lb = lb_ref[...]
    q = qz * _sigmoid(qz)
    sig = _sigmoid(fz)
    k = (1.0 - lb) * (1.0 - sig)
    g = jnp.log(lb + (1.0 - lb) * sig)

    def to_lanes(a):
        return jnp.concatenate([a[c * C:(c + 1) * C] for c in range(nch)], axis=1)

    q2, k2, g2, v2 = to_lanes(q), to_lanes(k), to_lanes(g), to_lanes(v)

    ri = lax.broadcasted_iota(jnp.int32, (C, C), 0)
    ci = lax.broadcasted_iota(jnp.int32, (C, C), 1)
    tri = (ri >= ci).astype(F32)
    b2 = jnp.dot(tri, g2, precision=lax.Precision.HIGHEST, preferred_element_type=F32)
    b_last = b2[C - 1:C]

    qe = (q2 * jnp.exp(b2)).astype(BF16)
    kdec = (k2 * jnp.exp(b_last - b2)).astype(BF16)

    refs = [jnp.broadcast_to(b2[SUB * i - 1:SUB * i], (SUB, width)) for i in range(1, nsub)]
    rb = jnp.concatenate([jnp.zeros((SUB, width), F32)] + refs, axis=0)
    qp = (q2 * jnp.exp(b2 - rb)).astype(BF16)
    kst, vst = [], []
    for i in range(1, nsub):
        ref_i = jnp.concatenate([refs[i - 1]] * i, axis=0)
        kst.append(k2[:SUB * i] * jnp.exp(ref_i - b2[:SUB * i]))
        vst.append(v2[:SUB * i])
    kst = jnp.concatenate(kst, axis=0).astype(BF16)
    vst = jnp.concatenate(vst, axis=0).astype(BF16)
    nstack = kst.shape[0]
    row_blk = lax.broadcasted_iota(jnp.int32, (C, nstack), 0) // SUB
    col = lax.broadcasted_iota(jnp.int32, (C, nstack), 1)
    col_blk = jnp.ones((C, nstack), jnp.int32)
    for i in range(2, nsub):
        col_blk = col_blk + (col >= (SUB * i * (i - 1)) // 2).astype(jnp.int32)
    stack_mask = row_blk == col_blk

    trow = lax.broadcasted_iota(jnp.int32, (C, width), 0) % SUB
    ps, vds = [], []
    for d in range(SUB):
        if d == 0:
            ps.append(q2 * k2)
            vds.append(v2)
            continue
        valid = trow >= d
        kd = pltpu.roll(k2, d, 0)
        bd = pltpu.roll(b2, d, 0)
        vds.append(pltpu.roll(v2, d, 0))
        e = jnp.where(valid, b2 - bd, 0.0)
        ps.append(jnp.where(valid, q2 * kd * jnp.exp(e), 0.0))
    li = lax.broadcasted_iota(jnp.int32, (MXU_DIM, MXU_DIM), 0) // DK
    lj = lax.broadcasted_iota(jnp.int32, (MXU_DIM, MXU_DIM), 1) // DK
    group_ones = (li == lj).astype(BF16)
    o_diag = []
    for p0 in range(0, width, MXU_DIM):
        lhs = jnp.concatenate([p[:, p0:p0 + MXU_DIM] for p in ps], axis=0).astype(BF16)
        a = _dot(lhs, group_ones)
        acc = a[:C] * vds[0][:, p0:p0 + MXU_DIM]
        for d in range(1, SUB):
            acc = acc + a[d * C:(d + 1) * C] * vds[d][:, p0:p0 + MXU_DIM]
        o_diag.append(acc)
    o_diag = jnp.concatenate(o_diag, axis=1)

    st = st_ref[...]
    decay_last = jnp.exp(b_last)
    outs = []
    for c in range(nch):
        sl = slice(c * DK, (c + 1) * DK)
        o_c = _dot_nt(qe[:, sl], st.astype(BF16))
        r = _dot_nt(qp[:, sl], kst[:, sl])
        r = jnp.where(stack_mask, r, 0.0).astype(BF16)
        o_c = o_c + _dot(r, vst[:, sl]) + o_diag[:, sl]
        st = st * decay_last[:, sl] + _dot(v2[:, sl].T.astype(BF16), kdec[:, sl])
        outs.append(o_c)
    st_ref[...] = st
    o = jnp.concatenate(outs, axis=0)
    o = o * lax.rsqrt(jnp.mean(o * o, axis=-1, keepdims=True) + RMS_EPS)
    o_ref[...] = (o * ng_ref[...] * (gz * _sigmoid(gz))).astype(BF16)


def _hgrn(xb, lb, w_in, norm_g):
    S, D = xb.shape
    H, DK = HGRN_HEADS, HGRN_DK
    T = min(ROW_TILE, S)
    w = w_in.astype(BF16).reshape(D, 4, H, DK).transpose(2, 0, 1, 3).reshape(H, D, 4 * DK)
    return pl.pallas_call(
        functools.partial(_hgrn_kernel, rows=T),
        grid=(H, S // T),
        in_specs=[
            pl.BlockSpec((T, D), lambda h, t: (t, 0)),
            pl.BlockSpec((None, D, 4 * DK), lambda h, t: (h, 0, 0)),
            pl.BlockSpec((None, 1, DK), lambda h, t: (h, 0, 0)),
            pl.BlockSpec((None, 1, DK), lambda h, t: (h, 0, 0)),
        ],
        out_specs=pl.BlockSpec((T, DK), lambda h, t: (t, h)),
        out_shape=jax.ShapeDtypeStruct((S, H * DK), BF16),
        scratch_shapes=[pltpu.VMEM((DK, DK), F32)],
        compiler_params=_params("arbitrary", "arbitrary"),
        name="hgrn2",
    )(xb, w, lb.reshape(H, 1, DK), norm_g.reshape(H, 1, DK))


def _qkv_kernel(xb_ref, posr_ref, posc_ref, wqt_ref, wk_ref, wvt_ref, fcol_ref, frow_ref,
                qt_ref, k_ref, vt_ref, *, rows):
    H, HD, L = ATTN_HEADS, HEAD_DIM, MOBA_BLOCK
    x = xb_ref[...]
    qt = _dot_nt(wqt_ref[...], x)
    vt = _dot_nt(wvt_ref[...], x)
    k = _dot(x, wk_ref[...])

    ang_t = fcol_ref[...] * posr_ref[...].astype(F32)
    cos_t, sin_t = jnp.cos(ang_t), jnp.sin(ang_t)
    ang = posc_ref[...].astype(F32) * frow_ref[...]
    cos_r, sin_r = jnp.cos(ang), jnp.sin(ang)
    lane = lax.broadcasted_iota(jnp.int32, (rows, HD), 1)
    first = lane < ROPE_HALF
    sin_r = jnp.where(first, -sin_r, sin_r)

    for h in range(H):
        blk = qt[h * HD:(h + 1) * HD]
        x1, x2 = blk[:ROPE_HALF], blk[ROPE_HALF:ROPE_DIM]
        qh = jnp.concatenate([x1 * cos_t - x2 * sin_t, x2 * cos_t + x1 * sin_t, blk[ROPE_DIM:]], axis=0)
        qh = (qh * Q_SCALE).astype(BF16)
        kh = k[:, h * HD:(h + 1) * HD]
        partner = jnp.where(first, pltpu.roll(kh, HD - ROPE_HALF, 1), pltpu.roll(kh, ROPE_HALF, 1))
        kh = (kh * cos_r + partner * sin_r).astype(BF16)
        vh = vt[h * HD:(h + 1) * HD].astype(BF16)
        for n in range(rows // L):
            qt_ref[h, n] = qh[:, n * L:(n + 1) * L]
            vt_ref[h, n] = vh[:, n * L:(n + 1) * L]
            k_ref[h, n] = kh[n * L:(n + 1) * L]


def _attn_kernel(qt_ref, k_ref, vt_ref, o_ref, kmean_ref, sel_ref, *, nblk):
    L = MOBA_BLOCK
    i = pl.program_id(1)

    @pl.when(i == 0)
    def _():
        def mean_body(n, carry):
            kmean_ref[pl.ds(n, 1), :] = jnp.mean(k_ref[n].astype(F32), axis=0, keepdims=True)
            return carry
        lax.fori_loop(0, nblk, mean_body, 0)

    qt = qt_ref[...]
    km = kmean_ref[...]
    km_hi = km.astype(BF16)
    km_lo = (km - km_hi.astype(F32)).astype(BF16)
    gate = _dot(km_hi, qt) + _dot(km_lo, qt)
    rowj = lax.broadcasted_iota(jnp.int32, (nblk, L), 0)
    gate = jnp.where(rowj < i, gate, -jnp.inf)
    sel = jnp.zeros((nblk, L), F32)
    for kk in range(MOBA_TOPK):
        mx = jnp.max(gate, axis=0, keepdims=True)
        idx = jnp.min(jnp.where(gate == mx, rowj, nblk), axis=0, keepdims=True)
        pick = jnp.logical_and(rowj == idx, kk < i)
        sel = jnp.where(pick, 1.0, sel)
        gate = jnp.where(pick, -jnp.inf, gate)
    sel_ref[...] = sel

    s = _dot(k_ref[i], qt)
    kpos = lax.broadcasted_iota(jnp.int32, (L, L), 0)
    qpos = lax.broadcasted_iota(jnp.int32, (L, L), 1)
    s = jnp.where(kpos <= qpos, s, MASK_NEG)
    m0 = jnp.max(s, axis=0, keepdims=True)
    p = jnp.exp2(s - m0)
    l0 = jnp.sum(p, axis=0, keepdims=True)
    acc0 = _dot(vt_ref[i], p.astype(BF16))

    def body(j, carry):
        m, l, acc = carry
        s = _dot(k_ref[j], qt)
        s = jnp.where(sel_ref[pl.ds(j, 1), :] > 0.5, s, MASK_NEG)
        m_new = jnp.maximum(m, jnp.max(s, axis=0, keepdims=True))
        alpha = jnp.exp2(m - m_new)
        p = jnp.exp2(s - m_new)
        l = alpha * l + jnp.sum(p, axis=0, keepdims=True)
        acc = alpha * acc + _dot(vt_ref[j], p.astype(BF16))
        return m_new, l, acc

    _, l, acc = lax.fori_loop(0, i, body, (m0, l0, acc0))
    o_ref[...] = (acc / l).T.astype(BF16)


def _moba(xb, positions, w_in):
    S, D = xb.shape
    H, HD, L = ATTN_HEADS, HEAD_DIM, MOBA_BLOCK
    assert S % L == 0
    NB = S // L
    T = min(ROW_TILE, S)
    nper = T // L
    wb = w_in.astype(BF16)
    wqt, wk, wvt = wb[:, :D].T, wb[:, D:2 * D], wb[:, 2 * D:].T
    inv_freq = 1.0 / (ROPE_THETA ** (jnp.arange(0, ROPE_DIM, 2, dtype=F32) / ROPE_DIM))
    fcol = inv_freq.reshape(ROPE_HALF, 1)
    frow = jnp.concatenate([inv_freq, inv_freq, jnp.zeros((HD - ROPE_DIM,), F32)]).reshape(1, HD)
    pos = positions.reshape(S).astype(jnp.int32)
    qt, k, vt = pl.pallas_call(
        functools.partial(_qkv_kernel, rows=T),
        grid=(S // T,),
        in_specs=[
            pl.BlockSpec((T, D), lambda i: (i, 0)),
            pl.BlockSpec((1, T), lambda i: (0, i)),
            pl.BlockSpec((T, 1), lambda i: (i, 0)),
            _const_spec((D, D)),
            _const_spec((D, D)),
            _const_spec((D, D)),
            _const_spec((ROPE_HALF, 1)),
            _const_spec((1, HD)),
        ],
        out_specs=[
            pl.BlockSpec((H, nper, HD, L), lambda i: (0, i, 0, 0)),
            pl.BlockSpec((H, nper, L, HD), lambda i: (0, i, 0, 0)),
            pl.BlockSpec((H, nper, HD, L), lambda i: (0, i, 0, 0)),
        ],
        out_shape=[
            jax.ShapeDtypeStruct((H, NB, HD, L), BF16),
            jax.ShapeDtypeStruct((H, NB, L, HD), BF16),
            jax.ShapeDtypeStruct((H, NB, HD, L), BF16),
        ],
        compiler_params=_params("arbitrary"),
        name="moba_qkv_rope",
    )(xb, pos.reshape(1, S), pos.reshape(S, 1), wqt, wk, wvt, fcol, frow)

    return pl.pallas_call(
        functools.partial(_attn_kernel, nblk=NB),
        grid=(H, NB),
        in_specs=[
            pl.BlockSpec((None, None, HD, L), lambda h, i: (h, i, 0, 0)),
            pl.BlockSpec((None, NB, L, HD), lambda h, i: (h, 0, 0, 0)),
            pl.BlockSpec((None, NB, HD, L), lambda h, i: (h, 0, 0, 0)),
        ],
        out_specs=pl.BlockSpec((L, HD), lambda h, i: (i, h)),
        out_shape=jax.ShapeDtypeStruct((S, H * HD), BF16),
        scratch_shapes=[pltpu.VMEM((NB, HD), F32), pltpu.VMEM((NB, L), F32)],
        compiler_params=_params("arbitrary", "arbitrary"),
        name="moba_attention",
    )(qt, k, vt)


def _hgrn_lower_bound(lower_bounds, layer):
    s = jax.nn.softmax(lower_bounds.astype(F32), axis=0)
    return (jnp.cumsum(s, axis=0) - s[0])[layer]


def kernel(x, positions, hgrn_lower_bounds, l0_mix_w_in, l0_mix_norm_g, l0_mix_w_out, l0_ln1_g, l0_ln1_b, l0_ffn_w_up, l0_ffn_conv, l0_ffn_w_down, l0_ln2_g, l0_ln2_b, l1_mix_w_in, l1_mix_w_out, l1_ln1_g, l1_ln1_b, l1_ffn_w_up, l1_ffn_conv, l1_ffn_w_down, l1_ln2_g, l1_ln2_b, l2_mix_w_in, l2_mix_conv, l2_mix_w_out, l2_ln1_g, l2_ln1_b, l2_ffn_w_up, l2_ffn_conv, l2_ffn_w_down, l2_ln2_g, l2_ln2_b, l3_mix_w_in, l3_mix_norm_g, l3_mix_w_out, l3_ln1_g, l3_ln1_b, l3_ffn_w_up, l3_ffn_conv, l3_ffn_w_down, l3_ln2_g, l3_ln2_b):
    layers = [
        ((l0_mix_w_in, l0_mix_norm_g, l0_mix_w_out), l0_ln1_g, l0_ln1_b,
         (l0_ffn_w_up, l0_ffn_conv, l0_ffn_w_down), l0_ln2_g, l0_ln2_b),
        ((l1_mix_w_in, l1_mix_w_out), l1_ln1_g, l1_ln1_b,
         (l1_ffn_w_up, l1_ffn_conv, l1_ffn_w_down), l1_ln2_g, l1_ln2_b),
        ((l2_mix_w_in, l2_mix_conv, l2_mix_w_out), l2_ln1_g, l2_ln1_b,
         (l2_ffn_w_up, l2_ffn_conv, l2_ffn_w_down), l2_ln2_g, l2_ln2_b),
        ((l3_mix_w_in, l3_mix_norm_g, l3_mix_w_out), l3_ln1_g, l3_ln1_b,
         (l3_ffn_w_up, l3_ffn_conv, l3_ffn_w_down), l3_ln2_g, l3_ln2_b),
    ]
    B, S, D = x.shape
    outs = []
    for bi in range(B):
        xf = x[bi]
        xb = xf.astype(BF16)
        for i in range(DEPTH):
            mix_p, ln1_g, ln1_b, ffn_p, ln2_g, ln2_b = layers[i]
            kind = i % N_MIXERS
            if kind == 0:
                w_in, norm_g, w_out = mix_p
                o = _hgrn(xb, _hgrn_lower_bound(hgrn_lower_bounds, i), w_in, norm_g)
                xf, xb = _proj_ln(o, w_out, xf, ln1_g, ln1_b)
            elif kind == 1:
                w_in, w_out = mix_p
                o = _moba(xb, positions[bi], w_in)
                xf, xb = _proj_ln(o, w_out, xf, ln1_g, ln1_b)
            else:
                w_in, conv_w, w_out = mix_p
                xf, xb = _conv_mixer(xb, xf, w_in, conv_w, w_out, ln1_g, ln1_b)
            xf, xb = _ffn(xb, xf, *ffn_p, ln2_g, ln2_b)
        outs.append(xf)
    return outs[0][None] if B == 1 else jnp.stack(outs, axis=0)
```

```python
import functools
import math

import jax
import jax.numpy as jnp
from jax import lax
from jax.experimental import pallas as pl
from jax.experimental.pallas import tpu as pltpu

F32 = jnp.float32
BF16 = jnp.bfloat16

DEPTH = 4
N_MIXERS = 3
HGRN_HEADS = 8
HGRN_DK = 128
ATTN_HEADS = 8
HEAD_DIM = 128
MOBA_BLOCK = 256
MOBA_TOPK = 3
ROPE_THETA = 500000.0
ROPE_DIM = HEAD_DIM // 4
ROPE_HALF = ROPE_DIM // 2
LN_EPS = 1e-5
RMS_EPS = 1e-6
DEEPNORM_ALPHA = (2.0 * DEPTH) ** 0.25
MASK_NEG = -1e30

V7X_VMEM_BYTES = 64 * 1024 * 1024
VMEM_LIMIT_BYTES = V7X_VMEM_BYTES - 8 * 1024 * 1024
SUBLANES = 8
LANES = 128
MXU_DIM = 256

ROW_TILE = 512
HGRN_CHUNK = 64
HGRN_SUB = SUBLANES
ATTN_GROUP = 8
Q_SCALE = (1.0 / math.sqrt(HEAD_DIM)) * math.log2(math.e)


def _params(*sem):
    return pltpu.CompilerParams(dimension_semantics=sem, vmem_limit_bytes=VMEM_LIMIT_BYTES)


def _const_spec(shape):
    nd = len(shape)
    return pl.BlockSpec(shape, lambda *_: (0,) * nd, pipeline_mode=pl.Buffered(1))


def _sigmoid(x):
    return 1.0 / (1.0 + jnp.exp(-x))


def _layer_norm(y, g, b):
    mu = jnp.mean(y, axis=-1, keepdims=True)
    yc = y - mu
    var = jnp.mean(yc * yc, axis=-1, keepdims=True)
    return yc * lax.rsqrt(var + LN_EPS) * g + b


def _dot(a, b):
    return jnp.dot(a, b, preferred_element_type=F32)


def _dot_nt(a, b):
    return lax.dot_general(a, b, (((1,), (1,)), ((), ())), preferred_element_type=F32)


def _causal_conv3(u, w, carry_ref, col0):
    width = u.shape[1]
    cols = slice(col0, col0 + width)
    row = lax.broadcasted_iota(jnp.int32, (SUBLANES, width), 0)
    y = u * w[2:3]
    for j in (1, 2):
        r = pltpu.roll(u, j, 0)
        prev = carry_ref[j - 1, :, cols]
        carry_ref[j - 1, :, cols] = r[:SUBLANES]
        head = jnp.where(row < j, prev, r[:SUBLANES])
        y = y + jnp.concatenate([head, r[SUBLANES:]], axis=0) * w[2 - j:3 - j]
    return y


def _proj_ln_kernel(a_ref, w_ref, x_ref, g_ref, b_ref, of_ref, ob_ref):
    m = _dot(a_ref[...], w_ref[...])
    y = _layer_norm(DEEPNORM_ALPHA * x_ref[...] + m, g_ref[...], b_ref[...])
    of_ref[...] = y
    ob_ref[...] = y.astype(BF16)


def _proj_ln(a, w, x, g, b):
    S, D = x.shape
    K = a.shape[1]
    T = min(ROW_TILE, S)
    return pl.pallas_call(
        _proj_ln_kernel,
        grid=(S // T,),
        in_specs=[
            pl.BlockSpec((T, K), lambda i: (i, 0)),
            _const_spec((K, D)),
            pl.BlockSpec((T, D), lambda i: (i, 0)),
            _const_spec((1, D)),
            _const_spec((1, D)),
        ],
        out_specs=[pl.BlockSpec((T, D), lambda i: (i, 0)), pl.BlockSpec((T, D), lambda i: (i, 0))],
        out_shape=[jax.ShapeDtypeStruct((S, D), F32), jax.ShapeDtypeStruct((S, D), BF16)],
        compiler_params=_params("arbitrary"),
        name="proj_ln",
    )(a, w.astype(BF16), x, g.reshape(1, D), b.reshape(1, D))


def _ffn_kernel(xb_ref, xf_ref, wup_ref, cw_ref, wdn_ref, g_ref, b_ref, of_ref, ob_ref,
                carry_ref, h_ref, *, d_ff, tile):
    @pl.when(pl.program_id(0) == 0)
    def _():
        carry_ref[...] = jnp.zeros_like(carry_ref)

    xb = xb_ref[...]
    for c in range(d_ff // tile):
        ys = []
        for part in range(2):
            col0 = part * d_ff + c * tile
            u = _dot(xb, wup_ref[:, col0:col0 + tile])
            ys.append(_causal_conv3(u, cw_ref[:, col0:col0 + tile], carry_ref, col0))
        ya, yb = ys
        h_ref[:, c * tile:(c + 1) * tile] = (ya * _sigmoid(ya) * yb).astype(BF16)
    m = _dot(h_ref[...], wdn_ref[...])
    y = _layer_norm(DEEPNORM_ALPHA * xf_ref[...] + m, g_ref[...], b_ref[...])
    of_ref[...] = y
    ob_ref[...] = y.astype(BF16)


def _ffn(xb, xf, w_up, conv_w, w_down, g, b):
    S, D = xf.shape
    d_ff = w_down.shape[0]
    T = min(ROW_TILE, S)
    tile = MXU_DIM
    assert d_ff % tile == 0
    return pl.pallas_call(
        functools.partial(_ffn_kernel, d_ff=d_ff, tile=tile),
        grid=(S // T,),
        in_specs=[
            pl.BlockSpec((T, D), lambda i: (i, 0)),
            pl.BlockSpec((T, D), lambda i: (i, 0)),
            _const_spec((D, 2 * d_ff)),
            _const_spec((3, 2 * d_ff)),
            _const_spec((d_ff, D)),
            _const_spec((1, D)),
            _const_spec((1, D)),
        ],
        out_specs=[pl.BlockSpec((T, D), lambda i: (i, 0)), pl.BlockSpec((T, D), lambda i: (i, 0))],
        out_shape=[jax.ShapeDtypeStruct((S, D), F32), jax.ShapeDtypeStruct((S, D), BF16)],
        scratch_shapes=[
            pltpu.VMEM((2, SUBLANES, 2 * d_ff), F32),
            pltpu.VMEM((T, d_ff), BF16),
        ],
        compiler_params=_params("arbitrary"),
        name="conv_ffn",
    )(xb, xf, w_up.astype(BF16), conv_w, w_down.astype(BF16), g.reshape(1, D), b.reshape(1, D))


def _conv_mixer_kernel(xb_ref, xf_ref, win_ref, cw_ref, wout_ref, g_ref, b_ref, of_ref, ob_ref,
                       carry_ref, y_ref, *, d_model, tile):
    @pl.when(pl.program_id(0) == 0)
    def _():
        carry_ref[...] = jnp.zeros_like(carry_ref)

    xb = xb_ref[...]
    for c in range(d_model // tile):
        col0 = c * tile
        bg = _dot(xb, win_ref[:, col0:col0 + tile])
        cg = _dot(xb, win_ref[:, d_model + col0:d_model + col0 + tile])
        hh = _dot(xb, win_ref[:, 2 * d_model + col0:2 * d_model + col0 + tile])
        conv = _causal_conv3(cg * hh, cw_ref[:, col0:col0 + tile], carry_ref, col0)
        y_ref[:, col0:col0 + tile] = (bg * conv).astype(BF16)
    m = _dot(y_ref[...], wout_ref[...])
    y = _layer_norm(DEEPNORM_ALPHA * xf_ref[...] + m, g_ref[...], b_ref[...])
    of_ref[...] = y
    ob_ref[...] = y.astype(BF16)


def _conv_mixer(xb, xf, w_in, conv_w, w_out, g, b):
    S, D = xf.shape
    T = min(ROW_TILE, S)
    return pl.pallas_call(
        functools.partial(_conv_mixer_kernel, d_model=D, tile=MXU_DIM),
        grid=(S // T,),
        in_specs=[
            pl.BlockSpec((T, D), lambda i: (i, 0)),
            pl.BlockSpec((T, D), lambda i: (i, 0)),
            _const_spec((D, 3 * D)),
            _const_spec((3, D)),
            _const_spec((D, D)),
            _const_spec((1, D)),
            _const_spec((1, D)),
        ],
        out_specs=[pl.BlockSpec((T, D), lambda i: (i, 0)), pl.BlockSpec((T, D), lambda i: (i, 0))],
        out_shape=[jax.ShapeDtypeStruct((S, D), F32), jax.ShapeDtypeStruct((S, D), BF16)],
        scratch_shapes=[
            pltpu.VMEM((2, SUBLANES, D), F32),
            pltpu.VMEM((T, D), BF16),
        ],
        compiler_params=_params("arbitrary"),
        name="conv_mixer",
    )(xb, xf, w_in.astype(BF16), conv_w, w_out.astype(BF16), g.reshape(1, D), b.reshape(1, D))


def _hgrn_kernel(xb_ref, w_ref, lb_ref, ng_ref, o_ref, st_ref, *, rows):
    C, SUB, DK = HGRN_CHUNK, HGRN_SUB, HGRN_DK
    nch = rows // C
    nsub = C // SUB
    width = nch * DK

    @pl.when(pl.program_id(1) == 0)
    def _():
        st_ref[...] = jnp.zeros_like(st_ref)

    proj = _dot(xb_ref[...], w_ref[...])
    qz, fz, v, gz = (proj[:, j * DK:(j + 1) * DK] for j in range(4))
    lb = lb_ref[...]
    q = qz * _sigmoid(qz)
    sig = _sigmoid(fz)
    k = (1.0 - lb) * (1.0 - sig)
    g = jnp.log(lb + (1.0 - lb) * sig)

    def to_lanes(a):
        return jnp.concatenate([a[c * C:(c + 1) * C] for c in range(nch)], axis=1)

    q2, k2, g2, v2 = to_lanes(q), to_lanes(k), to_lanes(g), to_lanes(v)

    ri = lax.broadcasted_iota(jnp.int32, (C, C), 0)
    ci = lax.broadcasted_iota(jnp.int32, (C, C), 1)
    tri = (ri >= ci).astype(F32)
    b2 = jnp.dot(tri, g2, precision=lax.Precision.HIGHEST, preferred_element_type=F32)
    b_last = b2[C - 1:C]

    qe = (q2 * jnp.exp(b2)).astype(BF16)
    kdec = (k2 * jnp.exp(b_last - b2)).astype(BF16)

    refs = [jnp.broadcast_to(b2[SUB * i - 1:SUB * i], (SUB, width)) for i in range(1, nsub)]
    rb = jnp.concatenate([jnp.zeros((SUB, width), F32)] + refs, axis=0)
    qp = (q2 * jnp.exp(b2 - rb)).astype(BF16)
    kst, vst = [], []
    for i in range(1, nsub):
        ref_i = jnp.concatenate([refs[i - 1]] * i, axis=0)
        kst.append(k2[:SUB * i] * jnp.exp(ref_i - b2[:SUB * i]))
        vst.append(v2[:SUB * i])
    kst = jnp.concatenate(kst, axis=0).astype(BF16)
    vst = jnp.concatenate(vst, axis=0).astype(BF16)
    nstack = kst.shape[0]
    row_blk = lax.broadcasted_iota(jnp.int32, (C, nstack), 0) // SUB
    col = lax.broadcasted_iota(jnp.int32, (C, nstack), 1)
    col_blk = jnp.ones((C, nstack), jnp.int32)
    for i in range(2, nsub):
        col_blk = col_blk + (col >= (SUB * i * (i - 1)) // 2).astype(jnp.int32)
    stack_mask = row_blk == col_blk

    trow = lax.broadcasted_iota(jnp.int32, (C, width), 0) % SUB
    ps, vds = [], []
    for d in range(SUB):
        if d == 0:
            ps.append(q2 * k2)
            vds.append(v2)
            continue
        valid = trow >= d
        kd = pltpu.roll(k2, d, 0)
        bd = pltpu.roll(b2, d, 0)
        vds.append(pltpu.roll(v2, d, 0))
        e = jnp.where(valid, b2 - bd, 0.0)
        ps.append(jnp.where(valid, q2 * kd * jnp.exp(e), 0.0))
    li = lax.broadcasted_iota(jnp.int32, (MXU_DIM, MXU_DIM), 0) // DK
    lj = lax.broadcasted_iota(jnp.int32, (MXU_DIM, MXU_DIM), 1) // DK
    group_ones = (li == lj).astype(BF16)
    o_diag = []
    for p0 in range(0, width, MXU_DIM):
        lhs = jnp.concatenate([p[:, p0:p0 + MXU_DIM] for p in ps], axis=0).astype(BF16)
        a = _dot(lhs, group_ones)
        acc = a[:C] * vds[0][:, p0:p0 + MXU_DIM]
        for d in range(1, SUB):
            acc = acc + a[d * C:(d + 1) * C] * vds[d][:, p0:p0 + MXU_DIM]
        o_diag.append(acc)
    o_diag = jnp.concatenate(o_diag, axis=1)

    st = st_ref[...]
    decay_last = jnp.exp(b_last)
    outs = []
    for c in range(nch):
        sl = slice(c * DK, (c + 1) * DK)
        o_c = _dot_nt(qe[:, sl], st.astype(BF16))
        r = _dot_nt(qp[:, sl], kst[:, sl])
        r = jnp.where(stack_mask, r, 0.0).astype(BF16)
        o_c = o_c + _dot(r, vst[:, sl]) + o_diag[:, sl]
        st = st * decay_last[:, sl] + _dot(v2[:, sl].T.astype(BF16), kdec[:, sl])
        outs.append(o_c)
    st_ref[...] = st
    o = jnp.concatenate(outs, axis=0)
    o = o * lax.rsqrt(jnp.mean(o * o, axis=-1, keepdims=True) + RMS_EPS)
    o_ref[...] = (o * ng_ref[...] * (gz * _sigmoid(gz))).astype(BF16)


def _hgrn(xb, lb, w_in, norm_g):
    S, D = xb.shape
    H, DK = HGRN_HEADS, HGRN_DK
    T = min(ROW_TILE, S)
    w = w_in.astype(BF16).reshape(D, 4, H, DK).transpose(2, 0, 1, 3).reshape(H, D, 4 * DK)
    return pl.pallas_call(
        functools.partial(_hgrn_kernel, rows=T),
        grid=(H, S // T),
        in_specs=[
            pl.BlockSpec((T, D), lambda h, t: (t, 0)),
            pl.BlockSpec((None, D, 4 * DK), lambda h, t: (h, 0, 0)),
            pl.BlockSpec((None, 1, DK), lambda h, t: (h, 0, 0)),
            pl.BlockSpec((None, 1, DK), lambda h, t: (h, 0, 0)),
        ],
        out_specs=pl.BlockSpec((T, DK), lambda h, t: (t, h)),
        out_shape=jax.ShapeDtypeStruct((S, H * DK), BF16),
        scratch_shapes=[pltpu.VMEM((DK, DK), F32)],
        compiler_params=_params("arbitrary", "arbitrary"),
        name="hgrn2",
    )(xb, w, lb.reshape(H, 1, DK), norm_g.reshape(H, 1, DK))


def _qkv_kernel(xb_ref, posr_ref, posc_ref, wqt_ref, wk_ref, wvt_ref, fcol_ref, frow_ref,
                qt_ref, k_ref, vt_ref, *, rows):
    H, HD, L = ATTN_HEADS, HEAD_DIM, MOBA_BLOCK
    x = xb_ref[...]
    qt = _dot_nt(wqt_ref[...], x)
    vt = _dot_nt(wvt_ref[...], x)
    k = _dot(x, wk_ref[...])

    ang_t = fcol_ref[...] * posr_ref[...].astype(F32)
    cos_t, sin_t = jnp.cos(ang_t), jnp.sin(ang_t)
    ang = posc_ref[...].astype(F32) * frow_ref[...]
    cos_r, sin_r = jnp.cos(ang), jnp.sin(ang)
    lane = lax.broadcasted_iota(jnp.int32, (rows, HD), 1)
    first = lane < ROPE_HALF
    sin_r = jnp.where(first, -sin_r, sin_r)

    for h in range(H):
        blk = qt[h * HD:(h + 1) * HD]
        x1, x2 = blk[:ROPE_HALF], blk[ROPE_HALF:ROPE_DIM]
        qh = jnp.concatenate([x1 * cos_t - x2 * sin_t, x2 * cos_t + x1 * sin_t, blk[ROPE_DIM:]], axis=0)
        qh = (qh * Q_SCALE).astype(BF16)
        kh = k[:, h * HD:(h + 1) * HD]
        partner = jnp.where(first, pltpu.roll(kh, HD - ROPE_HALF, 1), pltpu.roll(kh, ROPE_HALF, 1))
        kh = (kh * cos_r + partner * sin_r).astype(BF16)
        vh = vt[h * HD:(h + 1) * HD].astype(BF16)
        for n in range(rows // L):
            qt_ref[h, n] = qh[:, n * L:(n + 1) * L]
            vt_ref[h, n] = vh[:, n * L:(n + 1) * L]
            k_ref[h, n] = kh[n * L:(n + 1) * L]


def _attn_kernel(qt_ref, k_ref, vt_ref, o_ref, kmean_ref, sel_ref, s_ref, acc_ref, *, nblk):
    L = MOBA_BLOCK
    i = pl.program_id(1)

    @pl.when(i == 0)
    def _():
        def mean_body(n, carry):
            kmean_ref[pl.ds(n, 1), :] = jnp.mean(k_ref[n].astype(F32), axis=0, keepdims=True)
            return carry
        lax.fori_loop(0, nblk, mean_body, 0)

    qt = qt_ref[...]
    km = kmean_ref[...]
    km_hi = km.astype(BF16)
    km_lo = (km - km_hi.astype(F32)).astype(BF16)
    gate = _dot(km_hi, qt) + _dot(km_lo, qt)
    rowj = lax.broadcasted_iota(jnp.int32, (nblk, L), 0)
    gate = jnp.where(rowj < i, gate, -jnp.inf)
    sel = jnp.zeros((nblk, L), F32)
    for kk in range(MOBA_TOPK):
        mx = jnp.max(gate, axis=0, keepdims=True)
        idx = jnp.min(jnp.where(gate == mx, rowj, nblk), axis=0, keepdims=True)
        pick = jnp.logical_and(rowj == idx, kk < i)
        sel = jnp.where(pick, 1.0, sel)
        gate = jnp.where(pick, -jnp.inf, gate)
    sel_ref[:nblk] = sel

    @pl.when(i == 0)
    def _():
        sel_ref[nblk:] = jnp.zeros((sel_ref.shape[0] - nblk, L), F32)

    G = ATTN_GROUP
    own = nblk + G - 1
    ngrp = (i + G - 1) // G

    def group_max(s):
        return jnp.max(s.reshape(L // SUBLANES, SUBLANES, L), axis=0)

    def group_sum(p):
        return jnp.sum(p.reshape(L // SUBLANES, SUBLANES, L), axis=0)

    s = _dot(k_ref[i], qt)
    kpos = lax.broadcasted_iota(jnp.int32, (L, L), 0)
    qpos = lax.broadcasted_iota(jnp.int32, (L, L), 1)
    s = jnp.where(kpos <= qpos, s, MASK_NEG)
    s_ref[own] = s

    def score_body(t, m8):
        for u in range(G):
            jj = t * G + u
            s = _dot(k_ref[jnp.minimum(jj, nblk - 1)], qt)
            s = jnp.where(sel_ref[pl.ds(jj, 1), :] > 0.5, s, MASK_NEG)
            s_ref[jj] = s
            m8 = jnp.maximum(m8, group_max(s))
        return m8

    m8 = lax.fori_loop(0, ngrp, score_body, group_max(s))
    m = jnp.max(m8, axis=0, keepdims=True)

    p = jnp.exp2(s_ref[own] - m)
    acc_ref[...] = _dot(vt_ref[i], p.astype(BF16))

    def pv_body(t, l8):
        ps, vs = [], []
        for u in range(G):
            jj = t * G + u
            p = jnp.exp2(s_ref[jj] - m)
            l8 = l8 + group_sum(p)
            ps.append(p.astype(BF16))
            vs.append(vt_ref[jnp.minimum(jj, nblk - 1)])
        acc_ref[...] += _dot(jnp.concatenate(vs, axis=1), jnp.concatenate(ps, axis=0))
        return l8

    l8 = lax.fori_loop(0, ngrp, pv_body, group_sum(p))
    l = jnp.sum(l8, axis=0, keepdims=True)
    o_ref[...] = (acc_ref[...] / l).T.astype(BF16)


def _moba(xb, positions, w_in):
    S, D = xb.shape
    H, HD, L = ATTN_HEADS, HEAD_DIM, MOBA_BLOCK
    assert S % L == 0
    NB = S // L
    T = min(ROW_TILE, S)
    nper = T // L
    wb = w_in.astype(BF16)
    wqt, wk, wvt = wb[:, :D].T, wb[:, D:2 * D], wb[:, 2 * D:].T
    inv_freq = 1.0 / (ROPE_THETA ** (jnp.arange(0, ROPE_DIM, 2, dtype=F32) / ROPE_DIM))
    fcol = inv_freq.reshape(ROPE_HALF, 1)
    frow = jnp.concatenate([inv_freq, inv_freq, jnp.zeros((HD - ROPE_DIM,), F32)]).reshape(1, HD)
    pos = positions.reshape(S).astype(jnp.int32)
    qt, k, vt = pl.pallas_call(
        functools.partial(_qkv_kernel, rows=T),
        grid=(S // T,),
        in_specs=[
            pl.BlockSpec((T, D), lambda i: (i, 0)),
            pl.BlockSpec((1, T), lambda i: (0, i)),
            pl.BlockSpec((T, 1), lambda i: (i, 0)),
            _const_spec((D, D)),
            _const_spec((D, D)),
            _const_spec((D, D)),
            _const_spec((ROPE_HALF, 1)),
            _const_spec((1, HD)),
        ],
        out_specs=[
            pl.BlockSpec((H, nper, HD, L), lambda i: (0, i, 0, 0)),
            pl.BlockSpec((H, nper, L, HD), lambda i: (0, i, 0, 0)),
            pl.BlockSpec((H, nper, HD, L), lambda i: (0, i, 0, 0)),
        ],
        out_shape=[
            jax.ShapeDtypeStruct((H, NB, HD, L), BF16),
            jax.ShapeDtypeStruct((H, NB, L, HD), BF16),
            jax.ShapeDtypeStruct((H, NB, HD, L), BF16),
        ],
        compiler_params=_params("arbitrary"),
        name="moba_qkv_rope",
    )(xb, pos.reshape(1, S), pos.reshape(S, 1), wqt, wk, wvt, fcol, frow)

    return pl.pallas_call(
        functools.partial(_attn_kernel, nblk=NB),
        grid=(H, NB),
        in_specs=[
            pl.BlockSpec((None, None, HD, L), lambda h, i: (h, i, 0, 0)),
            pl.BlockSpec((None, NB, L, HD), lambda h, i: (h, 0, 0, 0)),
            pl.BlockSpec((None, NB, HD, L), lambda h, i: (h, 0, 0, 0)),
        ],
        out_specs=pl.BlockSpec((L, HD), lambda h, i: (i, h)),
        out_shape=jax.ShapeDtypeStruct((S, H * HD), BF16),
        scratch_shapes=[
            pltpu.VMEM((NB, HD), F32),
            pltpu.VMEM((NB + SUBLANES, L), F32),
            pltpu.VMEM((NB + ATTN_GROUP, L, L), F32),
            pltpu.VMEM((HD, L), F32),
        ],
        compiler_params=_params("arbitrary", "arbitrary"),
        name="moba_attention",
    )(qt, k, vt)


def _hgrn_lower_bound(lower_bounds, layer):
    s = jax.nn.softmax(lower_bounds.astype(F32), axis=0)
    return (jnp.cumsum(s, axis=0) - s[0])[layer]


def kernel(x, positions, hgrn_lower_bounds, l0_mix_w_in, l0_mix_norm_g, l0_mix_w_out, l0_ln1_g, l0_ln1_b, l0_ffn_w_up, l0_ffn_conv, l0_ffn_w_down, l0_ln2_g, l0_ln2_b, l1_mix_w_in, l1_mix_w_out, l1_ln1_g, l1_ln1_b, l1_ffn_w_up, l1_ffn_conv, l1_ffn_w_down, l1_ln2_g, l1_ln2_b, l2_mix_w_in, l2_mix_conv, l2_mix_w_out, l2_ln1_g, l2_ln1_b, l2_ffn_w_up, l2_ffn_conv, l2_ffn_w_down, l2_ln2_g, l2_ln2_b, l3_mix_w_in, l3_mix_norm_g, l3_mix_w_out, l3_ln1_g, l3_ln1_b, l3_ffn_w_up, l3_ffn_conv, l3_ffn_w_down, l3_ln2_g, l3_ln2_b):
    layers = [
        ((l0_mix_w_in, l0_mix_norm_g, l0_mix_w_out), l0_ln1_g, l0_ln1_b,
         (l0_ffn_w_up, l0_ffn_conv, l0_ffn_w_down), l0_ln2_g, l0_ln2_b),
        ((l1_mix_w_in, l1_mix_w_out), l1_ln1_g, l1_ln1_b,
         (l1_ffn_w_up, l1_ffn_conv, l1_ffn_w_down), l1_ln2_g, l1_ln2_b),
        ((l2_mix_w_in, l2_mix_conv, l2_mix_w_out), l2_ln1_g, l2_ln1_b,
         (l2_ffn_w_up, l2_ffn_conv, l2_ffn_w_down), l2_ln2_g, l2_ln2_b),
        ((l3_mix_w_in, l3_mix_norm_g, l3_mix_w_out), l3_ln1_g, l3_ln1_b,
         (l3_ffn_w_up, l3_ffn_conv, l3_ffn_w_down), l3_ln2_g, l3_ln2_b),
    ]
    B, S, D = x.shape
    outs = []
    for bi in range(B):
        xf = x[bi]
        xb = xf.astype(BF16)
        for i in range(DEPTH):
            mix_p, ln1_g, ln1_b, ffn_p, ln2_g, ln2_b = layers[i]
            kind = i % N_MIXERS
            if kind == 0:
                w_in, norm_g, w_out = mix_p
                o = _hgrn(xb, _hgrn_lower_bound(hgrn_lower_bounds, i), w_in, norm_g)
                xf, xb = _proj_ln(o, w_out, xf, ln1_g, ln1_b)
            elif kind == 1:
                w_in, w_out = mix_p
                o = _moba(xb, positions[bi], w_in)
                xf, xb = _proj_ln(o, w_out, xf, ln1_g, ln1_b)
            else:
                w_in, conv_w, w_out = mix_p
                xf, xb = _conv_mixer(xb, xf, w_in, conv_w, w_out, ln1_g, ln1_b)
            xf, xb = _ffn(xb, xf, *ffn_p, ln2_g, ln2_b)
        outs.append(xf)
    return outs[0][None] if B == 1 else jnp.stack(outs, axis=0)
```

```python
import functools
import math

import jax
import jax.numpy as jnp
from jax import lax
from jax.experimental import pallas as pl
from jax.experimental.pallas import tpu as pltpu

F32 = jnp.float32
BF16 = jnp.bfloat16

DEPTH = 4
N_MIXERS = 3
HGRN_HEADS = 8
HGRN_DK = 128
ATTN_HEADS = 8
HEAD_DIM = 128
MOBA_BLOCK = 256
MOBA_TOPK = 3
ROPE_THETA = 500000.0
ROPE_DIM = HEAD_DIM // 4
ROPE_HALF = ROPE_DIM // 2
LN_EPS = 1e-5
RMS_EPS = 1e-6
DEEPNORM_ALPHA = (2.0 * DEPTH) ** 0.25
MASK_NEG = -1e30

V7X_VMEM_BYTES = 64 * 1024 * 1024
VMEM_LIMIT_BYTES = V7X_VMEM_BYTES - 8 * 1024 * 1024
SUBLANES = 8
LANES = 128
MXU_DIM = 256

ROW_TILE = 512
HGRN_CHUNK = 64
HGRN_SUB = SUBLANES
HGRN_MAX_CHUNK_DECAY = 80.0
HGRN_HEADS_PER_STEP = 2
ATTN_GROUP = 8
Q_SCALE = (1.0 / math.sqrt(HEAD_DIM)) * math.log2(math.e)


def _params(*sem):
    return pltpu.CompilerParams(dimension_semantics=sem, vmem_limit_bytes=VMEM_LIMIT_BYTES)


def _const_spec(shape):
    nd = len(shape)
    return pl.BlockSpec(shape, lambda *_: (0,) * nd, pipeline_mode=pl.Buffered(1))


def _sigmoid(x):
    return 1.0 / (1.0 + jnp.exp(-x))


def _layer_norm(y, g, b):
    mu = jnp.mean(y, axis=-1, keepdims=True)
    yc = y - mu
    var = jnp.mean(yc * yc, axis=-1, keepdims=True)
    return yc * lax.rsqrt(var + LN_EPS) * g + b


def _dot(a, b):
    return jnp.dot(a, b, preferred_element_type=F32)


def _dot_nt(a, b):
    return lax.dot_general(a, b, (((1,), (1,)), ((), ())), preferred_element_type=F32)


def _causal_conv3(u, w, carry_ref, col0):
    width = u.shape[1]
    cols = slice(col0, col0 + width)
    row = lax.broadcasted_iota(jnp.int32, (SUBLANES, width), 0)
    y = u * w[2:3]
    for j in (1, 2):
        r = pltpu.roll(u, j, 0)
        prev = carry_ref[j - 1, :, cols]
        carry_ref[j - 1, :, cols] = r[:SUBLANES]
        head = jnp.where(row < j, prev, r[:SUBLANES])
        y = y + jnp.concatenate([head, r[SUBLANES:]], axis=0) * w[2 - j:3 - j]
    return y


def _proj_ln_kernel(a_ref, w_ref, x_ref, g_ref, b_ref, of_ref, ob_ref):
    m = _dot(a_ref[...], w_ref[...])
    y = _layer_norm(DEEPNORM_ALPHA * x_ref[...] + m, g_ref[...], b_ref[...])
    of_ref[...] = y
    ob_ref[...] = y.astype(BF16)


def _proj_ln(a, w, x, g, b):
    S, D = x.shape
    K = a.shape[1]
    T = min(ROW_TILE, S)
    return pl.pallas_call(
        _proj_ln_kernel,
        grid=(S // T,),
        in_specs=[
            pl.BlockSpec((T, K), lambda i: (i, 0)),
            _const_spec((K, D)),
            pl.BlockSpec((T, D), lambda i: (i, 0)),
            _const_spec((1, D)),
            _const_spec((1, D)),
        ],
        out_specs=[pl.BlockSpec((T, D), lambda i: (i, 0)), pl.BlockSpec((T, D), lambda i: (i, 0))],
        out_shape=[jax.ShapeDtypeStruct((S, D), F32), jax.ShapeDtypeStruct((S, D), BF16)],
        compiler_params=_params("arbitrary"),
        name="proj_ln",
    )(a, w.astype(BF16), x, g.reshape(1, D), b.reshape(1, D))


def _ffn_kernel(xb_ref, xf_ref, wup_ref, cw_ref, wdn_ref, g_ref, b_ref, of_ref, ob_ref,
                carry_ref, h_ref, *, d_ff, tile):
    @pl.when(pl.program_id(0) == 0)
    def _():
        carry_ref[...] = jnp.zeros_like(carry_ref)

    xb = xb_ref[...]
    for c in range(d_ff // tile):
        ys = []
        for part in range(2):
            col0 = part * d_ff + c * tile
            u = _dot(xb, wup_ref[:, col0:col0 + tile])
            ys.append(_causal_conv3(u, cw_ref[:, col0:col0 + tile], carry_ref, col0))
        ya, yb = ys
        h_ref[:, c * tile:(c + 1) * tile] = (ya * _sigmoid(ya) * yb).astype(BF16)
    m = _dot(h_ref[...], wdn_ref[...])
    y = _layer_norm(DEEPNORM_ALPHA * xf_ref[...] + m, g_ref[...], b_ref[...])
    of_ref[...] = y
    ob_ref[...] = y.astype(BF16)


def _ffn(xb, xf, w_up, conv_w, w_down, g, b):
    S, D = xf.shape
    d_ff = w_down.shape[0]
    T = min(ROW_TILE, S)
    tile = MXU_DIM
    assert d_ff % tile == 0
    return pl.pallas_call(
        functools.partial(_ffn_kernel, d_ff=d_ff, tile=tile),
        grid=(S // T,),
        in_specs=[
            pl.BlockSpec((T, D), lambda i: (i, 0)),
            pl.BlockSpec((T, D), lambda i: (i, 0)),
            _const_spec((D, 2 * d_ff)),
            _const_spec((3, 2 * d_ff)),
            _const_spec((d_ff, D)),
            _const_spec((1, D)),
            _const_spec((1, D)),
        ],
        out_specs=[pl.BlockSpec((T, D), lambda i: (i, 0)), pl.BlockSpec((T, D), lambda i: (i, 0))],
        out_shape=[jax.ShapeDtypeStruct((S, D), F32), jax.ShapeDtypeStruct((S, D), BF16)],
        scratch_shapes=[
            pltpu.VMEM((2, SUBLANES, 2 * d_ff), F32),
            pltpu.VMEM((T, d_ff), BF16),
        ],
        compiler_params=_params("arbitrary"),
        name="conv_ffn",
    )(xb, xf, w_up.astype(BF16), conv_w, w_down.astype(BF16), g.reshape(1, D), b.reshape(1, D))


def _conv_mixer_kernel(xb_ref, xf_ref, win_ref, cw_ref, wout_ref, g_ref, b_ref, of_ref, ob_ref,
                       carry_ref, y_ref, *, d_model, tile):
    @pl.when(pl.program_id(0) == 0)
    def _():
        carry_ref[...] = jnp.zeros_like(carry_ref)

    xb = xb_ref[...]
    for c in range(d_model // tile):
        col0 = c * tile
        bg = _dot(xb, win_ref[:, col0:col0 + tile])
        cg = _dot(xb, win_ref[:, d_model + col0:d_model + col0 + tile])
        hh = _dot(xb, win_ref[:, 2 * d_model + col0:2 * d_model + col0 + tile])
        conv = _causal_conv3(cg * hh, cw_ref[:, col0:col0 + tile], carry_ref, col0)
        y_ref[:, col0:col0 + tile] = (bg * conv).astype(BF16)
    m = _dot(y_ref[...], wout_ref[...])
    y = _layer_norm(DEEPNORM_ALPHA * xf_ref[...] + m, g_ref[...], b_ref[...])
    of_ref[...] = y
    ob_ref[...] = y.astype(BF16)


def _conv_mixer(xb, xf, w_in, conv_w, w_out, g, b):
    S, D = xf.shape
    T = min(ROW_TILE, S)
    return pl.pallas_call(
        functools.partial(_conv_mixer_kernel, d_model=D, tile=MXU_DIM),
        grid=(S // T,),
        in_specs=[
            pl.BlockSpec((T, D), lambda i: (i, 0)),
            pl.BlockSpec((T, D), lambda i: (i, 0)),
            _const_spec((D, 3 * D)),
            _const_spec((3, D)),
            _const_spec((D, D)),
            _const_spec((1, D)),
            _const_spec((1, D)),
        ],
        out_specs=[pl.BlockSpec((T, D), lambda i: (i, 0)), pl.BlockSpec((T, D), lambda i: (i, 0))],
        out_shape=[jax.ShapeDtypeStruct((S, D), F32), jax.ShapeDtypeStruct((S, D), BF16)],
        scratch_shapes=[
            pltpu.VMEM((2, SUBLANES, D), F32),
            pltpu.VMEM((T, D), BF16),
        ],
        compiler_params=_params("arbitrary"),
        name="conv_mixer",
    )(xb, xf, w_in.astype(BF16), conv_w, w_out.astype(BF16), g.reshape(1, D), b.reshape(1, D))


def _hgrn_prepare(xb, w, lb, rows):
    C, DK = HGRN_CHUNK, HGRN_DK
    nch = rows // C

    proj = _dot(xb, w)
    qz, fz, v, gz = (proj[:, j * DK:(j + 1) * DK] for j in range(4))
    q = qz * _sigmoid(qz)
    sig = _sigmoid(fz)
    k = (1.0 - lb) * (1.0 - sig)
    g = jnp.log(lb + (1.0 - lb) * sig)

    def to_lanes(a):
        return jnp.concatenate([a[c * C:(c + 1) * C] for c in range(nch)], axis=1)

    q2, k2, g2, v2 = to_lanes(q), to_lanes(k), to_lanes(g), to_lanes(v)

    ri = lax.broadcasted_iota(jnp.int32, (C, C), 0)
    ci = lax.broadcasted_iota(jnp.int32, (C, C), 1)
    tri = (ri >= ci).astype(BF16)
    g_hi = g2.astype(BF16)
    g_rem = g2 - g_hi.astype(F32)
    g_mid = g_rem.astype(BF16)
    g_lo = (g_rem - g_mid.astype(F32)).astype(BF16)
    b2 = _dot(tri, g_hi) + _dot(tri, g_mid) + _dot(tri, g_lo)
    return dict(q2=q2, k2=k2, v2=v2, b2=b2, gz=gz)


def _hgrn_chunk_decay_columns(b2):
    C, DK = HGRN_CHUNK, HGRN_DK
    nch = b2.shape[1] // DK
    b_last = b2[C - 1:C]
    rows = [b_last[:, c * DK:(c + 1) * DK] for c in range(nch)]
    padded = jnp.concatenate(rows + [jnp.zeros((DK - nch, DK), F32)], axis=0)
    return jnp.exp(padded.T)


def _hgrn_chunks_factored(p, st):
    C, DK = HGRN_CHUNK, HGRN_DK
    q2, k2, v2, b2 = p["q2"], p["k2"], p["v2"], p["b2"]
    qe = (q2 * jnp.exp(b2)).astype(BF16)
    kinv = k2 * jnp.exp(-b2)
    kinv_b = kinv.astype(BF16)
    v_b = v2.astype(BF16)
    causal = lax.broadcasted_iota(jnp.int32, (C, C), 0) >= lax.broadcasted_iota(jnp.int32, (C, C), 1)
    decay = _hgrn_chunk_decay_columns(b2)
    outs = []
    for c in range(q2.shape[1] // DK):
        sl = slice(c * DK, (c + 1) * DK)
        o_inter = _dot(qe[:, sl], st.astype(BF16))
        a = jnp.where(causal, _dot_nt(qe[:, sl], kinv_b[:, sl]), 0.0).astype(BF16)
        both = _dot(jnp.concatenate([a, kinv[:, sl].T.astype(BF16)], axis=0), v_b[:, sl])
        outs.append(o_inter + both[:C])
        st = decay[:, c:c + 1] * (st + both[C:])
    return jnp.concatenate(outs, axis=0), st


def _hgrn_chunks_robust(p, st):
    C, SUB, DK = HGRN_CHUNK, HGRN_SUB, HGRN_DK
    nsub = C // SUB
    q2, k2, v2, b2 = p["q2"], p["k2"], p["v2"], p["b2"]
    width = q2.shape[1]
    b_last = b2[C - 1:C]
    qe = (q2 * jnp.exp(b2)).astype(BF16)
    kdec = k2 * jnp.exp(b_last - b2)
    v_b = v2.astype(BF16)
    refs = [jnp.broadcast_to(b2[SUB * i - 1:SUB * i], (SUB, width)) for i in range(1, nsub)]
    rb = jnp.concatenate([jnp.zeros((SUB, width), F32)] + refs, axis=0)
    qp = (q2 * jnp.exp(b2 - rb)).astype(BF16)
    kst, vst = [], []
    for i in range(1, nsub):
        ref_i = jnp.concatenate([refs[i - 1]] * i, axis=0)
        kst.append(k2[:SUB * i] * jnp.exp(ref_i - b2[:SUB * i]))
        vst.append(v2[:SUB * i])
    kst = jnp.concatenate(kst, axis=0).astype(BF16)
    vst = jnp.concatenate(vst, axis=0).astype(BF16)
    nstack = kst.shape[0]
    row_blk = lax.broadcasted_iota(jnp.int32, (C, nstack), 0) // SUB
    col = lax.broadcasted_iota(jnp.int32, (C, nstack), 1)
    col_blk = jnp.ones((C, nstack), jnp.int32)
    for i in range(2, nsub):
        col_blk = col_blk + (col >= (SUB * i * (i - 1)) // 2).astype(jnp.int32)
    stack_mask = row_blk == col_blk
    o_diag = _hgrn_exact_diagonal(p)
    decay = _hgrn_chunk_decay_columns(b2)
    outs = []
    for c in range(width // DK):
        sl = slice(c * DK, (c + 1) * DK)
        r = jnp.where(stack_mask, _dot_nt(qp[:, sl], kst[:, sl]), 0.0).astype(BF16)
        outs.append(_dot(qe[:, sl], st.astype(BF16)) + _dot(r, vst[:, sl]) + o_diag[:, sl])
        st = decay[:, c:c + 1] * st + _dot(kdec[:, sl].T.astype(BF16), v_b[:, sl])
    return jnp.concatenate(outs, axis=0), st


def _hgrn_exact_diagonal(p):
    C, SUB, DK = HGRN_CHUNK, HGRN_SUB, HGRN_DK
    q2, k2, v2, b2 = p["q2"], p["k2"], p["v2"], p["b2"]
    width = q2.shape[1]
    trow = lax.broadcasted_iota(jnp.int32, (C, width), 0) % SUB
    ps, vds = [q2 * k2], [v2]
    for d in range(1, SUB):
        valid = trow >= d
        kd = pltpu.roll(k2, d, 0)
        bd = pltpu.roll(b2, d, 0)
        vds.append(pltpu.roll(v2, d, 0))
        e = jnp.where(valid, b2 - bd, 0.0)
        ps.append(jnp.where(valid, q2 * kd * jnp.exp(e), 0.0))
    li = lax.broadcasted_iota(jnp.int32, (MXU_DIM, MXU_DIM), 0) // DK
    lj = lax.broadcasted_iota(jnp.int32, (MXU_DIM, MXU_DIM), 1) // DK
    group_ones = (li == lj).astype(BF16)
    out = []
    for p0 in range(0, width, MXU_DIM):
        lhs = jnp.concatenate([x[:, p0:p0 + MXU_DIM] for x in ps], axis=0).astype(BF16)
        a = _dot(lhs, group_ones)
        acc = a[:C] * vds[0][:, p0:p0 + MXU_DIM]
        for d in range(1, SUB):
            acc = acc + a[d * C:(d + 1) * C] * vds[d][:, p0:p0 + MXU_DIM]
        out.append(acc)
    return jnp.concatenate(out, axis=1)


def _hgrn_kernel(xb_ref, w_ref, lb_ref, ng_ref, o_ref, st_ref, *, rows, heads):
    DK = HGRN_DK

    @pl.when(pl.program_id(1) == 0)
    def _():
        st_ref[...] = jnp.zeros_like(st_ref)

    xb = xb_ref[...]
    prep = [_hgrn_prepare(xb, w_ref[h], lb_ref[h], rows) for h in range(heads)]
    in_range = jnp.min(prep[0]["b2"]) >= -HGRN_MAX_CHUNK_DECAY
    for p in prep[1:]:
        in_range = jnp.logical_and(in_range, jnp.min(p["b2"]) >= -HGRN_MAX_CHUNK_DECAY)

    def run(chunks_fn):
        for h, p in enumerate(prep):
            o, st = chunks_fn(p, st_ref[h])
            st_ref[h] = st
            o = o * lax.rsqrt(jnp.mean(o * o, axis=-1, keepdims=True) + RMS_EPS)
            gz = p["gz"]
            o_ref[:, h * DK:(h + 1) * DK] = (o * ng_ref[h] * (gz * _sigmoid(gz))).astype(BF16)

    @pl.when(in_range)
    def _():
        run(_hgrn_chunks_factored)

    @pl.when(jnp.logical_not(in_range))
    def _():
        run(_hgrn_chunks_robust)


def _hgrn(xb, lb, w_in, norm_g):
    S, D = xb.shape
    H, DK, HP = HGRN_HEADS, HGRN_DK, HGRN_HEADS_PER_STEP
    T = min(ROW_TILE, S)
    w = w_in.astype(BF16).reshape(D, 4, H, DK).transpose(2, 0, 1, 3).reshape(H, D, 4 * DK)
    return pl.pallas_call(
        functools.partial(_hgrn_kernel, rows=T, heads=HP),
        grid=(H // HP, S // T),
        in_specs=[
            pl.BlockSpec((T, D), lambda h, t: (t, 0)),
            pl.BlockSpec((HP, D, 4 * DK), lambda h, t: (h, 0, 0)),
            pl.BlockSpec((HP, 1, DK), lambda h, t: (h, 0, 0)),
            pl.BlockSpec((HP, 1, DK), lambda h, t: (h, 0, 0)),
        ],
        out_specs=pl.BlockSpec((T, HP * DK), lambda h, t: (t, h)),
        out_shape=jax.ShapeDtypeStruct((S, H * DK), BF16),
        scratch_shapes=[pltpu.VMEM((HP, DK, DK), F32)],
        compiler_params=_params("arbitrary", "arbitrary"),
        name="hgrn2",
    )(xb, w, lb.reshape(H, 1, DK), norm_g.reshape(H, 1, DK))


def _qkv_kernel(xb_ref, posr_ref, posc_ref, wqt_ref, wk_ref, wvt_ref, fcol_ref, frow_ref,
                qt_ref, k_ref, vt_ref, *, rows):
    H, HD, L = ATTN_HEADS, HEAD_DIM, MOBA_BLOCK
    x = xb_ref[...]
    qt = _dot_nt(wqt_ref[...], x)
    vt = _dot_nt(wvt_ref[...], x)
    k = _dot(x, wk_ref[...])

    ang_t = fcol_ref[...] * posr_ref[...].astype(F32)
    cos_t, sin_t = jnp.cos(ang_t), jnp.sin(ang_t)
    ang = posc_ref[...].astype(F32) * frow_ref[...]
    cos_r, sin_r = jnp.cos(ang), jnp.sin(ang)
    lane = lax.broadcasted_iota(jnp.int32, (rows, HD), 1)
    first = lane < ROPE_HALF
    sin_r = jnp.where(first, -sin_r, sin_r)

    for h in range(H):
        blk = qt[h * HD:(h + 1) * HD]
        x1, x2 = blk[:ROPE_HALF], blk[ROPE_HALF:ROPE_DIM]
        qh = jnp.concatenate([x1 * cos_t - x2 * sin_t, x2 * cos_t + x1 * sin_t, blk[ROPE_DIM:]], axis=0)
        qh = (qh * Q_SCALE).astype(BF16)
        kh = k[:, h * HD:(h + 1) * HD]
        partner = jnp.where(first, pltpu.roll(kh, HD - ROPE_HALF, 1), pltpu.roll(kh, ROPE_HALF, 1))
        kh = (kh * cos_r + partner * sin_r).astype(BF16)
        vh = vt[h * HD:(h + 1) * HD].astype(BF16)
        for n in range(rows // L):
            qt_ref[h, n] = qh[:, n * L:(n + 1) * L]
            vt_ref[h, n] = vh[:, n * L:(n + 1) * L]
            k_ref[h, n] = kh[n * L:(n + 1) * L]


def _attn_kernel(qt_ref, k_ref, vt_ref, o_ref, kmean_ref, sel_ref, s_ref, acc_ref, *, nblk):
    L = MOBA_BLOCK
    i = pl.program_id(1)

    @pl.when(i == 0)
    def _():
        def mean_body(n, carry):
            kmean_ref[pl.ds(n, 1), :] = jnp.mean(k_ref[n].astype(F32), axis=0, keepdims=True)
            return carry
        lax.fori_loop(0, nblk, mean_body, 0)

    qt = qt_ref[...]
    km = kmean_ref[...]
    km_hi = km.astype(BF16)
    km_lo = (km - km_hi.astype(F32)).astype(BF16)
    gate = _dot(km_hi, qt) + _dot(km_lo, qt)
    rowj = lax.broadcasted_iota(jnp.int32, (nblk, L), 0)
    gate = jnp.where(rowj < i, gate, -jnp.inf)
    sel = jnp.zeros((nblk, L), F32)
    for kk in range(MOBA_TOPK):
        mx = jnp.max(gate, axis=0, keepdims=True)
        idx = jnp.min(jnp.where(gate == mx, rowj, nblk), axis=0, keepdims=True)
        pick = jnp.logical_and(rowj == idx, kk < i)
        sel = jnp.where(pick, 1.0, sel)
        gate = jnp.where(pick, -jnp.inf, gate)
    sel_ref[:nblk] = sel

    @pl.when(i == 0)
    def _():
        sel_ref[nblk:] = jnp.zeros((sel_ref.shape[0] - nblk, L), F32)

    G = ATTN_GROUP
    own = nblk + G - 1
    ngrp = (i + G - 1) // G

    def group_max(s):
        return jnp.max(s.reshape(L // SUBLANES, SUBLANES, L), axis=0)

    def group_sum(p):
        return jnp.sum(p.reshape(L // SUBLANES, SUBLANES, L), axis=0)

    s = _dot(k_ref[i], qt)
    kpos = lax.broadcasted_iota(jnp.int32, (L, L), 0)
    qpos = lax.broadcasted_iota(jnp.int32, (L, L), 1)
    s = jnp.where(kpos <= qpos, s, MASK_NEG)
    s_ref[own] = s

    def score_body(t, m8):
        for u in range(G):
            jj = t * G + u
            s = _dot(k_ref[jnp.minimum(jj, nblk - 1)], qt)
            s = jnp.where(sel_ref[pl.ds(jj, 1), :] > 0.5, s, MASK_NEG)
            s_ref[jj] = s
            m8 = jnp.maximum(m8, group_max(s))
        return m8

    m8 = lax.fori_loop(0, ngrp, score_body, group_max(s))
    m = jnp.max(m8, axis=0, keepdims=True)

    p = jnp.exp2(s_ref[own] - m)
    acc_ref[...] = _dot(vt_ref[i], p.astype(BF16))

    def pv_body(t, l8):
        ps, vs = [], []
        for u in range(G):
            jj = t * G + u
            p = jnp.exp2(s_ref[jj] - m)
            l8 = l8 + group_sum(p)
            ps.append(p.astype(BF16))
            vs.append(vt_ref[jnp.minimum(jj, nblk - 1)])
        acc_ref[...] += _dot(jnp.concatenate(vs, axis=1), jnp.concatenate(ps, axis=0))
        return l8

    l8 = lax.fori_loop(0, ngrp, pv_body, group_sum(p))
    l = jnp.sum(l8, axis=0, keepdims=True)
    o_ref[...] = (acc_ref[...] / l).T.astype(BF16)


def _moba(xb, positions, w_in):
    S, D = xb.shape
    H, HD, L = ATTN_HEADS, HEAD_DIM, MOBA_BLOCK
    assert S % L == 0
    NB = S // L
    T = min(ROW_TILE, S)
    nper = T // L
    wb = w_in.astype(BF16)
    wqt, wk, wvt = wb[:, :D].T, wb[:, D:2 * D], wb[:, 2 * D:].T
    inv_freq = 1.0 / (ROPE_THETA ** (jnp.arange(0, ROPE_DIM, 2, dtype=F32) / ROPE_DIM))
    fcol = inv_freq.reshape(ROPE_HALF, 1)
    frow = jnp.concatenate([inv_freq, inv_freq, jnp.zeros((HD - ROPE_DIM,), F32)]).reshape(1, HD)
    pos = positions.reshape(S).astype(jnp.int32)
    qt, k, vt = pl.pallas_call(
        functools.partial(_qkv_kernel, rows=T),
        grid=(S // T,),
        in_specs=[
            pl.BlockSpec((T, D), lambda i: (i, 0)),
            pl.BlockSpec((1, T), lambda i: (0, i)),
            pl.BlockSpec((T, 1), lambda i: (i, 0)),
            _const_spec((D, D)),
            _const_spec((D, D)),
            _const_spec((D, D)),
            _const_spec((ROPE_HALF, 1)),
            _const_spec((1, HD)),
        ],
        out_specs=[
            pl.BlockSpec((H, nper, HD, L), lambda i: (0, i, 0, 0)),
            pl.BlockSpec((H, nper, L, HD), lambda i: (0, i, 0, 0)),
            pl.BlockSpec((H, nper, HD, L), lambda i: (0, i, 0, 0)),
        ],
        out_shape=[
            jax.ShapeDtypeStruct((H, NB, HD, L), BF16),
            jax.ShapeDtypeStruct((H, NB, L, HD), BF16),
            jax.ShapeDtypeStruct((H, NB, HD, L), BF16),
        ],
        compiler_params=_params("arbitrary"),
        name="moba_qkv_rope",
    )(xb, pos.reshape(1, S), pos.reshape(S, 1), wqt, wk, wvt, fcol, frow)

    return pl.pallas_call(
        functools.partial(_attn_kernel, nblk=NB),
        grid=(H, NB),
        in_specs=[
            pl.BlockSpec((None, None, HD, L), lambda h, i: (h, i, 0, 0)),
            pl.BlockSpec((None, NB, L, HD), lambda h, i: (h, 0, 0, 0)),
            pl.BlockSpec((None, NB, HD, L), lambda h, i: (h, 0, 0, 0)),
        ],
        out_specs=pl.BlockSpec((L, HD), lambda h, i: (i, h)),
        out_shape=jax.ShapeDtypeStruct((S, H * HD), BF16),
        scratch_shapes=[
            pltpu.VMEM((NB, HD), F32),
            pltpu.VMEM((NB + SUBLANES, L), F32),
            pltpu.VMEM((NB + ATTN_GROUP, L, L), F32),
            pltpu.VMEM((HD, L), F32),
        ],
        compiler_params=_params("arbitrary", "arbitrary"),
        name="moba_attention",
    )(qt, k, vt)


def _hgrn_lower_bound(lower_bounds, layer):
    s = jax.nn.softmax(lower_bounds.astype(F32), axis=0)
    return (jnp.cumsum(s, axis=0) - s[0])[layer]


def kernel(x, positions, hgrn_lower_bounds, l0_mix_w_in, l0_mix_norm_g, l0_mix_w_out, l0_ln1_g, l0_ln1_b, l0_ffn_w_up, l0_ffn_conv, l0_ffn_w_down, l0_ln2_g, l0_ln2_b, l1_mix_w_in, l1_mix_w_out, l1_ln1_g, l1_ln1_b, l1_ffn_w_up, l1_ffn_conv, l1_ffn_w_down, l1_ln2_g, l1_ln2_b, l2_mix_w_in, l2_mix_conv, l2_mix_w_out, l2_ln1_g, l2_ln1_b, l2_ffn_w_up, l2_ffn_conv, l2_ffn_w_down, l2_ln2_g, l2_ln2_b, l3_mix_w_in, l3_mix_norm_g, l3_mix_w_out, l3_ln1_g, l3_ln1_b, l3_ffn_w_up, l3_ffn_conv, l3_ffn_w_down, l3_ln2_g, l3_ln2_b):
    layers = [
        ((l0_mix_w_in, l0_mix_norm_g, l0_mix_w_out), l0_ln1_g, l0_ln1_b,
         (l0_ffn_w_up, l0_ffn_conv, l0_ffn_w_down), l0_ln2_g, l0_ln2_b),
        ((l1_mix_w_in, l1_mix_w_out), l1_ln1_g, l1_ln1_b,
         (l1_ffn_w_up, l1_ffn_conv, l1_ffn_w_down), l1_ln2_g, l1_ln2_b),
        ((l2_mix_w_in, l2_mix_conv, l2_mix_w_out), l2_ln1_g, l2_ln1_b,
         (l2_ffn_w_up, l2_ffn_conv, l2_ffn_w_down), l2_ln2_g, l2_ln2_b),
        ((l3_mix_w_in, l3_mix_norm_g, l3_mix_w_out), l3_ln1_g, l3_ln1_b,
         (l3_ffn_w_up, l3_ffn_conv, l3_ffn_w_down), l3_ln2_g, l3_ln2_b),
    ]
    B, S, D = x.shape
    outs = []
    for bi in range(B):
        xf = x[bi]
        xb = xf.astype(BF16)
        for i in range(DEPTH):
            mix_p, ln1_g, ln1_b, ffn_p, ln2_g, ln2_b = layers[i]
            kind = i % N_MIXERS
            if kind == 0:
                w_in, norm_g, w_out = mix_p
                o = _hgrn(xb, _hgrn_lower_bound(hgrn_lower_bounds, i), w_in, norm_g)
                xf, xb = _proj_ln(o, w_out, xf, ln1_g, ln1_b)
            elif kind == 1:
                w_in, w_out = mix_p
                o = _moba(xb, positions[bi], w_in)
                xf, xb = _proj_ln(o, w_out, xf, ln1_g, ln1_b)
            else:
                w_in, conv_w, w_out = mix_p
                xf, xb = _conv_mixer(xb, xf, w_in, conv_w, w_out, ln1_g, ln1_b)
            xf, xb = _ffn(xb, xf, *ffn_p, ln2_g, ln2_b)
        outs.append(xf)
    return outs[0][None] if B == 1 else jnp.stack(outs, axis=0)
```

```python
import functools
import math

import jax
import jax.numpy as jnp
from jax import lax
from jax.experimental import pallas as pl
from jax.experimental.pallas import tpu as pltpu

F32 = jnp.float32
BF16 = jnp.bfloat16

DEPTH = 4
N_MIXERS = 3
HGRN_HEADS = 8
HGRN_DK = 128
ATTN_HEADS = 8
HEAD_DIM = 128
MOBA_BLOCK = 256
MOBA_TOPK = 3
ROPE_THETA = 500000.0
ROPE_DIM = HEAD_DIM // 4
ROPE_HALF = ROPE_DIM // 2
LN_EPS = 1e-5
RMS_EPS = 1e-6
DEEPNORM_ALPHA = (2.0 * DEPTH) ** 0.25
MASK_NEG = -1e30

V7X_VMEM_BYTES = 64 * 1024 * 1024
VMEM_LIMIT_BYTES = V7X_VMEM_BYTES - 8 * 1024 * 1024
SUBLANES = 8
BF16_SUBLANES = 16
LANES = 128
MXU_DIM = 256

ROW_TILE = 512
HGRN_CHUNK = 64
HGRN_SUB = SUBLANES
HGRN_MAX_CHUNK_DECAY = 80.0
HGRN_HEADS_PER_STEP = 2
ATTN_GROUP = 16
Q_SCALE = (1.0 / math.sqrt(HEAD_DIM)) * math.log2(math.e)


def _params(*sem):
    return pltpu.CompilerParams(dimension_semantics=sem, vmem_limit_bytes=VMEM_LIMIT_BYTES)


def _const_spec(shape):
    nd = len(shape)
    return pl.BlockSpec(shape, lambda *_: (0,) * nd, pipeline_mode=pl.Buffered(1))


def _sigmoid(x):
    return 1.0 / (1.0 + jnp.exp(-x))


def _layer_norm(y, g, b):
    mu = jnp.mean(y, axis=-1, keepdims=True)
    yc = y - mu
    var = jnp.mean(yc * yc, axis=-1, keepdims=True)
    return yc * lax.rsqrt(var + LN_EPS) * g + b


def _dot(a, b):
    return jnp.dot(a, b, preferred_element_type=F32)


def _dot_nt(a, b):
    return lax.dot_general(a, b, (((1,), (1,)), ((), ())), preferred_element_type=F32)


def _causal_conv3(u, w, carry_ref, col0):
    width = u.shape[1]
    cols = slice(col0, col0 + width)
    row = lax.broadcasted_iota(jnp.int32, (SUBLANES, width), 0)
    y = u * w[2:3]
    for j in (1, 2):
        r = pltpu.roll(u, j, 0)
        prev = carry_ref[j - 1, :, cols]
        carry_ref[j - 1, :, cols] = r[:SUBLANES]
        head = jnp.where(row < j, prev, r[:SUBLANES])
        y = y + jnp.concatenate([head, r[SUBLANES:]], axis=0) * w[2 - j:3 - j]
    return y


def _ffn_kernel(*refs, d_ff, tile, project):
    if project:
        (a_ref, wo_ref, g1_ref, b1_ref, xin_ref, wup_ref, cw_ref, wdn_ref, g_ref, b_ref,
         of_ref, ob_ref, carry_ref, h_ref, xf_ref) = refs
    else:
        (xb_ref, xf_ref, wup_ref, cw_ref, wdn_ref, g_ref, b_ref,
         of_ref, ob_ref, carry_ref, h_ref) = refs

    @pl.when(pl.program_id(0) == 0)
    def _():
        carry_ref[...] = jnp.zeros_like(carry_ref)

    if project:
        x = _layer_norm(DEEPNORM_ALPHA * xin_ref[...] + _dot(a_ref[...], wo_ref[...]),
                        g1_ref[...], b1_ref[...])
        xf_ref[...] = x
        xb = x.astype(BF16)
    else:
        xb = xb_ref[...]
    for c in range(d_ff // tile):
        ys = []
        for part in range(2):
            col0 = part * d_ff + c * tile
            u = _dot(xb, wup_ref[:, col0:col0 + tile])
            ys.append(_causal_conv3(u, cw_ref[:, col0:col0 + tile], carry_ref, col0))
        ya, yb = ys
        h_ref[:, c * tile:(c + 1) * tile] = (ya * _sigmoid(ya) * yb).astype(BF16)
    m = _dot(h_ref[...], wdn_ref[...])
    y = _layer_norm(DEEPNORM_ALPHA * xf_ref[...] + m, g_ref[...], b_ref[...])
    of_ref[...] = y
    ob_ref[...] = y.astype(BF16)


def _ffn(xf, w_up, conv_w, w_down, g, b, xb=None, mixer=None):
    S, D = xf.shape
    d_ff = w_down.shape[0]
    T = min(ROW_TILE, S)
    tile = MXU_DIM
    assert d_ff % tile == 0
    rows = lambda width: pl.BlockSpec((T, width), lambda i: (i, 0))
    vec = lambda v: v.reshape(1, D)
    ffn_specs = [_const_spec((D, 2 * d_ff)), _const_spec((3, 2 * d_ff)), _const_spec((d_ff, D)),
                 _const_spec((1, D)), _const_spec((1, D))]
    ffn_args = (w_up.astype(BF16), conv_w, w_down.astype(BF16), vec(g), vec(b))
    scratch = [pltpu.VMEM((2, SUBLANES, 2 * d_ff), F32), pltpu.VMEM((T, d_ff), BF16)]
    if mixer is None:
        in_specs = [rows(D), rows(D)] + ffn_specs
        args = (xb, xf) + ffn_args
    else:
        a, w_out, g1, b1 = mixer
        K = a.shape[1]
        in_specs = [rows(K), _const_spec((K, D)), _const_spec((1, D)), _const_spec((1, D)), rows(D)] + ffn_specs
        args = (a, w_out.astype(BF16), vec(g1), vec(b1), xf) + ffn_args
        scratch.append(pltpu.VMEM((T, D), F32))
    return pl.pallas_call(
        functools.partial(_ffn_kernel, d_ff=d_ff, tile=tile, project=mixer is not None),
        grid=(S // T,),
        in_specs=in_specs,
        out_specs=[rows(D), rows(D)],
        out_shape=[jax.ShapeDtypeStruct((S, D), F32), jax.ShapeDtypeStruct((S, D), BF16)],
        scratch_shapes=scratch,
        compiler_params=_params("arbitrary"),
        name="conv_ffn",
    )(*args)


def _conv_mixer_kernel(xb_ref, xf_ref, win_ref, cw_ref, wout_ref, g_ref, b_ref, of_ref, ob_ref,
                       carry_ref, y_ref, *, d_model, tile):
    @pl.when(pl.program_id(0) == 0)
    def _():
        carry_ref[...] = jnp.zeros_like(carry_ref)

    xb = xb_ref[...]
    for c in range(d_model // tile):
        col0 = c * tile
        bg = _dot(xb, win_ref[:, col0:col0 + tile])
        cg = _dot(xb, win_ref[:, d_model + col0:d_model + col0 + tile])
        hh = _dot(xb, win_ref[:, 2 * d_model + col0:2 * d_model + col0 + tile])
        conv = _causal_conv3(cg * hh, cw_ref[:, col0:col0 + tile], carry_ref, col0)
        y_ref[:, col0:col0 + tile] = (bg * conv).astype(BF16)
    m = _dot(y_ref[...], wout_ref[...])
    y = _layer_norm(DEEPNORM_ALPHA * xf_ref[...] + m, g_ref[...], b_ref[...])
    of_ref[...] = y
    ob_ref[...] = y.astype(BF16)


def _conv_mixer(xb, xf, w_in, conv_w, w_out, g, b):
    S, D = xf.shape
    T = min(ROW_TILE, S)
    return pl.pallas_call(
        functools.partial(_conv_mixer_kernel, d_model=D, tile=MXU_DIM),
        grid=(S // T,),
        in_specs=[
            pl.BlockSpec((T, D), lambda i: (i, 0)),
            pl.BlockSpec((T, D), lambda i: (i, 0)),
            _const_spec((D, 3 * D)),
            _const_spec((3, D)),
            _const_spec((D, D)),
            _const_spec((1, D)),
            _const_spec((1, D)),
        ],
        out_specs=[pl.BlockSpec((T, D), lambda i: (i, 0)), pl.BlockSpec((T, D), lambda i: (i, 0))],
        out_shape=[jax.ShapeDtypeStruct((S, D), F32), jax.ShapeDtypeStruct((S, D), BF16)],
        scratch_shapes=[
            pltpu.VMEM((2, SUBLANES, D), F32),
            pltpu.VMEM((T, D), BF16),
        ],
        compiler_params=_params("arbitrary"),
        name="conv_mixer",
    )(xb, xf, w_in.astype(BF16), conv_w, w_out.astype(BF16), g.reshape(1, D), b.reshape(1, D))


def _hgrn_prepare(xb, w, lb, rows):
    C, DK = HGRN_CHUNK, HGRN_DK
    nch = rows // C

    proj = _dot(xb, w)
    qz, fz, v, gz = (proj[:, j * DK:(j + 1) * DK] for j in range(4))
    q = qz * _sigmoid(qz)
    sig = _sigmoid(fz)
    k = (1.0 - lb) * (1.0 - sig)
    g = jnp.log(lb + (1.0 - lb) * sig)

    def to_lanes(a):
        return jnp.concatenate([a[c * C:(c + 1) * C] for c in range(nch)], axis=1)

    q2, k2, g2, v2 = to_lanes(q), to_lanes(k), to_lanes(g), to_lanes(v)

    ri = lax.broadcasted_iota(jnp.int32, (C, C), 0)
    ci = lax.broadcasted_iota(jnp.int32, (C, C), 1)
    tri = (ri >= ci).astype(BF16)
    g_hi = g2.astype(BF16)
    g_rem = g2 - g_hi.astype(F32)
    g_mid = g_rem.astype(BF16)
    g_lo = (g_rem - g_mid.astype(F32)).astype(BF16)
    b2 = _dot(tri, g_hi) + _dot(tri, g_mid) + _dot(tri, g_lo)
    return dict(q2=q2, k2=k2, v2=v2, b2=b2, gz=gz)


def _hgrn_chunk_decay_columns(b2):
    C, DK = HGRN_CHUNK, HGRN_DK
    nch = b2.shape[1] // DK
    b_last = b2[C - 1:C]
    rows = [b_last[:, c * DK:(c + 1) * DK] for c in range(nch)]
    padded = jnp.concatenate(rows + [jnp.zeros((DK - nch, DK), F32)], axis=0)
    return jnp.exp(padded.T)


def _hgrn_chunks_factored(p, st):
    C, DK = HGRN_CHUNK, HGRN_DK
    q2, k2, v2, b2 = p["q2"], p["k2"], p["v2"], p["b2"]
    qe = (q2 * jnp.exp(b2)).astype(BF16)
    kinv = k2 * jnp.exp(-b2)
    kinv_b = kinv.astype(BF16)
    v_b = v2.astype(BF16)
    causal = lax.broadcasted_iota(jnp.int32, (C, C), 0) >= lax.broadcasted_iota(jnp.int32, (C, C), 1)
    decay = _hgrn_chunk_decay_columns(b2)
    outs = []
    for c in range(q2.shape[1] // DK):
        sl = slice(c * DK, (c + 1) * DK)
        o_inter = _dot(qe[:, sl], st.astype(BF16))
        a = jnp.where(causal, _dot_nt(qe[:, sl], kinv_b[:, sl]), 0.0).astype(BF16)
        both = _dot(jnp.concatenate([a, kinv[:, sl].T.astype(BF16)], axis=0), v_b[:, sl])
        outs.append(o_inter + both[:C])
        st = decay[:, c:c + 1] * (st + both[C:])
    return jnp.concatenate(outs, axis=0), st


def _hgrn_chunks_robust(p, st):
    C, SUB, DK = HGRN_CHUNK, HGRN_SUB, HGRN_DK
    nsub = C // SUB
    q2, k2, v2, b2 = p["q2"], p["k2"], p["v2"], p["b2"]
    width = q2.shape[1]
    b_last = b2[C - 1:C]
    qe = (q2 * jnp.exp(b2)).astype(BF16)
    kdec = k2 * jnp.exp(b_last - b2)
    v_b = v2.astype(BF16)
    refs = [jnp.broadcast_to(b2[SUB * i - 1:SUB * i], (SUB, width)) for i in range(1, nsub)]
    rb = jnp.concatenate([jnp.zeros((SUB, width), F32)] + refs, axis=0)
    qp = (q2 * jnp.exp(b2 - rb)).astype(BF16)
    kst, vst = [], []
    for i in range(1, nsub):
        ref_i = jnp.concatenate([refs[i - 1]] * i, axis=0)
        kst.append(k2[:SUB * i] * jnp.exp(ref_i - b2[:SUB * i]))
        vst.append(v2[:SUB * i])
    kst = jnp.concatenate(kst, axis=0).astype(BF16)
    vst = jnp.concatenate(vst, axis=0).astype(BF16)
    nstack = kst.shape[0]
    row_blk = lax.broadcasted_iota(jnp.int32, (C, nstack), 0) // SUB
    col = lax.broadcasted_iota(jnp.int32, (C, nstack), 1)
    col_blk = jnp.ones((C, nstack), jnp.int32)
    for i in range(2, nsub):
        col_blk = col_blk + (col >= (SUB * i * (i - 1)) // 2).astype(jnp.int32)
    stack_mask = row_blk == col_blk
    o_diag = _hgrn_exact_diagonal(p)
    decay = _hgrn_chunk_decay_columns(b2)
    outs = []
    for c in range(width // DK):
        sl = slice(c * DK, (c + 1) * DK)
        r = jnp.where(stack_mask, _dot_nt(qp[:, sl], kst[:, sl]), 0.0).astype(BF16)
        outs.append(_dot(qe[:, sl], st.astype(BF16)) + _dot(r, vst[:, sl]) + o_diag[:, sl])
        st = decay[:, c:c + 1] * st + _dot(kdec[:, sl].T.astype(BF16), v_b[:, sl])
    return jnp.concatenate(outs, axis=0), st


def _hgrn_exact_diagonal(p):
    C, SUB, DK = HGRN_CHUNK, HGRN_SUB, HGRN_DK
    q2, k2, v2, b2 = p["q2"], p["k2"], p["v2"], p["b2"]
    width = q2.shape[1]
    trow = lax.broadcasted_iota(jnp.int32, (C, width), 0) % SUB
    ps, vds = [q2 * k2], [v2]
    for d in range(1, SUB):
        valid = trow >= d
        kd = pltpu.roll(k2, d, 0)
        bd = pltpu.roll(b2, d, 0)
        vds.append(pltpu.roll(v2, d, 0))
        e = jnp.where(valid, b2 - bd, 0.0)
        ps.append(jnp.where(valid, q2 * kd * jnp.exp(e), 0.0))
    li = lax.broadcasted_iota(jnp.int32, (MXU_DIM, MXU_DIM), 0) // DK
    lj = lax.broadcasted_iota(jnp.int32, (MXU_DIM, MXU_DIM), 1) // DK
    group_ones = (li == lj).astype(BF16)
    out = []
    for p0 in range(0, width, MXU_DIM):
        lhs = jnp.concatenate([x[:, p0:p0 + MXU_DIM] for x in ps], axis=0).astype(BF16)
        a = _dot(lhs, group_ones)
        acc = a[:C] * vds[0][:, p0:p0 + MXU_DIM]
        for d in range(1, SUB):
            acc = acc + a[d * C:(d + 1) * C] * vds[d][:, p0:p0 + MXU_DIM]
        out.append(acc)
    return jnp.concatenate(out, axis=1)


def _hgrn_kernel(xb_ref, w_ref, lb_ref, ng_ref, o_ref, st_ref, *, rows, heads):
    DK = HGRN_DK

    @pl.when(pl.program_id(1) == 0)
    def _():
        st_ref[...] = jnp.zeros_like(st_ref)

    xb = xb_ref[...]
    prep = [_hgrn_prepare(xb, w_ref[h], lb_ref[h], rows) for h in range(heads)]
    in_range = jnp.min(prep[0]["b2"]) >= -HGRN_MAX_CHUNK_DECAY
    for p in prep[1:]:
        in_range = jnp.logical_and(in_range, jnp.min(p["b2"]) >= -HGRN_MAX_CHUNK_DECAY)

    def run(chunks_fn):
        for h, p in enumerate(prep):
            o, st = chunks_fn(p, st_ref[h])
            st_ref[h] = st
            o = o * lax.rsqrt(jnp.mean(o * o, axis=-1, keepdims=True) + RMS_EPS)
            gz = p["gz"]
            o_ref[:, h * DK:(h + 1) * DK] = (o * ng_ref[h] * (gz * _sigmoid(gz))).astype(BF16)

    @pl.when(in_range)
    def _():
        run(_hgrn_chunks_factored)

    @pl.when(jnp.logical_not(in_range))
    def _():
        run(_hgrn_chunks_robust)


def _hgrn(xb, lb, w_in, norm_g):
    S, D = xb.shape
    H, DK, HP = HGRN_HEADS, HGRN_DK, HGRN_HEADS_PER_STEP
    T = min(ROW_TILE, S)
    w = w_in.astype(BF16).reshape(D, 4, H, DK).transpose(2, 0, 1, 3).reshape(H, D, 4 * DK)
    return pl.pallas_call(
        functools.partial(_hgrn_kernel, rows=T, heads=HP),
        grid=(H // HP, S // T),
        in_specs=[
            pl.BlockSpec((T, D), lambda h, t: (t, 0)),
            pl.BlockSpec((HP, D, 4 * DK), lambda h, t: (h, 0, 0)),
            pl.BlockSpec((HP, 1, DK), lambda h, t: (h, 0, 0)),
            pl.BlockSpec((HP, 1, DK), lambda h, t: (h, 0, 0)),
        ],
        out_specs=pl.BlockSpec((T, HP * DK), lambda h, t: (t, h)),
        out_shape=jax.ShapeDtypeStruct((S, H * DK), BF16),
        scratch_shapes=[pltpu.VMEM((HP, DK, DK), F32)],
        compiler_params=_params("arbitrary", "arbitrary"),
        name="hgrn2",
    )(xb, w, lb.reshape(H, 1, DK), norm_g.reshape(H, 1, DK))


def _qkv_kernel(xb_ref, posr_ref, posc_ref, wqt_ref, wk_ref, wvt_ref, fcol_ref, frow_ref,
                qt_ref, k_ref, vt_ref, *, rows):
    H, HD, L = ATTN_HEADS, HEAD_DIM, MOBA_BLOCK
    x = xb_ref[...]
    qt = _dot_nt(wqt_ref[...], x)
    vt = _dot_nt(wvt_ref[...], x)
    k = _dot(x, wk_ref[...])

    ang_t = fcol_ref[...] * posr_ref[...].astype(F32)
    cos_t, sin_t = jnp.cos(ang_t), jnp.sin(ang_t)
    ang = posc_ref[...].astype(F32) * frow_ref[...]
    cos_r, sin_r = jnp.cos(ang), jnp.sin(ang)
    lane = lax.broadcasted_iota(jnp.int32, (rows, HD), 1)
    first = lane < ROPE_HALF
    sin_r = jnp.where(first, -sin_r, sin_r)

    for h in range(H):
        blk = qt[h * HD:(h + 1) * HD]
        x1, x2 = blk[:ROPE_HALF], blk[ROPE_HALF:ROPE_DIM]
        qh = jnp.concatenate([x1 * cos_t - x2 * sin_t, x2 * cos_t + x1 * sin_t, blk[ROPE_DIM:]], axis=0)
        qh = (qh * Q_SCALE).astype(BF16)
        kh = k[:, h * HD:(h + 1) * HD]
        partner = jnp.where(first, pltpu.roll(kh, HD - ROPE_HALF, 1), pltpu.roll(kh, ROPE_HALF, 1))
        kh = (kh * cos_r + partner * sin_r).astype(BF16)
        vh = vt[h * HD:(h + 1) * HD].astype(BF16)
        for n in range(rows // L):
            qt_ref[h, n] = qh[:, n * L:(n + 1) * L]
            vt_ref[h, n] = vh[:, n * L:(n + 1) * L]
            k_ref[h, n] = kh[n * L:(n + 1) * L]


def _attn_kernel(qt_ref, k_ref, vt_ref, o_ref, kmean_ref, sel_ref, s_ref, acc_ref, *, nblk):
    L = MOBA_BLOCK
    i = pl.program_id(1)

    G = ATTN_GROUP
    nslots = sel_ref.shape[0]

    @pl.when(i == 0)
    def _():
        def mean_body(n, carry):
            kmean_ref[pl.ds(n, 1), :] = jnp.mean(k_ref[n].astype(F32), axis=0, keepdims=True)
            return carry
        lax.fori_loop(0, nblk, mean_body, 0)
        if nslots > nblk:
            sel_ref[nblk:] = jnp.zeros((nslots - nblk, L), F32)

    qt = qt_ref[...]
    km = kmean_ref[...]
    km_hi = km.astype(BF16)
    km_lo = (km - km_hi.astype(F32)).astype(BF16)
    gate = _dot(km_hi, qt) + _dot(km_lo, qt)

    kpos = lax.broadcasted_iota(jnp.int32, (L, L), 0)
    qpos = lax.broadcasted_iota(jnp.int32, (L, L), 1)
    s_own = jnp.where(kpos <= qpos, _dot(k_ref[i], qt), MASK_NEG)
    s_ref[i] = s_own

    rowj = lax.broadcasted_iota(jnp.int32, (nblk, L), 0)
    gate = jnp.where(rowj < i, gate, -jnp.inf)
    sel = jnp.zeros((nblk, L), F32)
    for kk in range(MOBA_TOPK):
        mx = jnp.max(gate, axis=0, keepdims=True)
        idx = jnp.min(jnp.where(gate == mx, rowj, nblk), axis=0, keepdims=True)
        pick = jnp.logical_and(rowj == idx, kk < i)
        sel = jnp.where(pick, 1.0, sel)
        gate = jnp.where(pick, -jnp.inf, gate)
    sel_ref[:nblk] = sel

    def group_max(s):
        return jnp.max(s.reshape(L // SUBLANES, SUBLANES, L), axis=0)

    ngrp = (i + G) // G

    def score_body(t, m8):
        for u in range(G):
            jj = t * G + u
            s = _dot(k_ref[jnp.minimum(jj, nblk - 1)], qt)
            s = jnp.where(sel_ref[pl.ds(jj, 1), :] > 0.5, s, MASK_NEG)
            s_ref[jnp.where(jj < i, jj, jj + 1)] = s
            m8 = jnp.maximum(m8, group_max(s))
        return m8

    m8 = lax.fori_loop(0, ngrp, score_body, group_max(s_own))
    m = jnp.max(m8, axis=0, keepdims=True)

    def probs(slot):
        return jnp.exp2(s_ref[slot] - m).astype(BF16)

    def with_ones(vt):
        return jnp.concatenate([vt, jnp.ones((BF16_SUBLANES, vt.shape[1]), BF16)], axis=0)

    acc_ref[...] = jnp.zeros_like(acc_ref)

    def pv_body(t, carry):
        total = None
        for u in range(G):
            slot = t * G + u
            part = _dot(with_ones(vt_ref[jnp.minimum(slot, nblk - 1)]), probs(slot))
            total = part if total is None else total + part
        acc_ref[...] += total
        return carry

    lax.fori_loop(0, ngrp, pv_body, 0)
    acc = acc_ref[...]
    o_ref[...] = (acc[:HEAD_DIM] / acc[HEAD_DIM:HEAD_DIM + 1]).T.astype(BF16)


def _moba(xb, positions, w_in):
    S, D = xb.shape
    H, HD, L = ATTN_HEADS, HEAD_DIM, MOBA_BLOCK
    assert S % L == 0
    NB = S // L
    T = min(ROW_TILE, S)
    nper = T // L
    nslots = pl.cdiv(NB, ATTN_GROUP) * ATTN_GROUP
    wb = w_in.astype(BF16)
    wqt, wk, wvt = wb[:, :D].T, wb[:, D:2 * D], wb[:, 2 * D:].T
    inv_freq = 1.0 / (ROPE_THETA ** (jnp.arange(0, ROPE_DIM, 2, dtype=F32) / ROPE_DIM))
    fcol = inv_freq.reshape(ROPE_HALF, 1)
    frow = jnp.concatenate([inv_freq, inv_freq, jnp.zeros((HD - ROPE_DIM,), F32)]).reshape(1, HD)
    pos = positions.reshape(S).astype(jnp.int32)
    qt, k, vt = pl.pallas_call(
        functools.partial(_qkv_kernel, rows=T),
        grid=(S // T,),
        in_specs=[
            pl.BlockSpec((T, D), lambda i: (i, 0)),
            pl.BlockSpec((1, T), lambda i: (0, i)),
            pl.BlockSpec((T, 1), lambda i: (i, 0)),
            _const_spec((D, D)),
            _const_spec((D, D)),
            _const_spec((D, D)),
            _const_spec((ROPE_HALF, 1)),
            _const_spec((1, HD)),
        ],
        out_specs=[
            pl.BlockSpec((H, nper, HD, L), lambda i: (0, i, 0, 0)),
            pl.BlockSpec((H, nper, L, HD), lambda i: (0, i, 0, 0)),
            pl.BlockSpec((H, nper, HD, L), lambda i: (0, i, 0, 0)),
        ],
        out_shape=[
            jax.ShapeDtypeStruct((H, NB, HD, L), BF16),
            jax.ShapeDtypeStruct((H, NB, L, HD), BF16),
            jax.ShapeDtypeStruct((H, NB, HD, L), BF16),
        ],
        compiler_params=_params("arbitrary"),
        name="moba_qkv_rope",
    )(xb, pos.reshape(1, S), pos.reshape(S, 1), wqt, wk, wvt, fcol, frow)

    return pl.pallas_call(
        functools.partial(_attn_kernel, nblk=NB),
        grid=(H, NB),
        in_specs=[
            pl.BlockSpec((None, None, HD, L), lambda h, i: (h, i, 0, 0)),
            pl.BlockSpec((None, NB, L, HD), lambda h, i: (h, 0, 0, 0)),
            pl.BlockSpec((None, NB, HD, L), lambda h, i: (h, 0, 0, 0)),
        ],
        out_specs=pl.BlockSpec((L, HD), lambda h, i: (i, h)),
        out_shape=jax.ShapeDtypeStruct((S, H * HD), BF16),
        scratch_shapes=[
            pltpu.VMEM((NB, HD), F32),
            pltpu.VMEM((nslots, L), F32),
            pltpu.VMEM((nslots + 1, L, L), F32),
            pltpu.VMEM((HD + BF16_SUBLANES, L), F32),
        ],
        compiler_params=_params("arbitrary", "arbitrary"),
        name="moba_attention",
    )(qt, k, vt)


def _hgrn_lower_bound(lower_bounds, layer):
    s = jax.nn.softmax(lower_bounds.astype(F32), axis=0)
    return (jnp.cumsum(s, axis=0) - s[0])[layer]


def kernel(x, positions, hgrn_lower_bounds, l0_mix_w_in, l0_mix_norm_g, l0_mix_w_out, l0_ln1_g, l0_ln1_b, l0_ffn_w_up, l0_ffn_conv, l0_ffn_w_down, l0_ln2_g, l0_ln2_b, l1_mix_w_in, l1_mix_w_out, l1_ln1_g, l1_ln1_b, l1_ffn_w_up, l1_ffn_conv, l1_ffn_w_down, l1_ln2_g, l1_ln2_b, l2_mix_w_in, l2_mix_conv, l2_mix_w_out, l2_ln1_g, l2_ln1_b, l2_ffn_w_up, l2_ffn_conv, l2_ffn_w_down, l2_ln2_g, l2_ln2_b, l3_mix_w_in, l3_mix_norm_g, l3_mix_w_out, l3_ln1_g, l3_ln1_b, l3_ffn_w_up, l3_ffn_conv, l3_ffn_w_down, l3_ln2_g, l3_ln2_b):
    layers = [
        ((l0_mix_w_in, l0_mix_norm_g, l0_mix_w_out), l0_ln1_g, l0_ln1_b,
         (l0_ffn_w_up, l0_ffn_conv, l0_ffn_w_down), l0_ln2_g, l0_ln2_b),
        ((l1_mix_w_in, l1_mix_w_out), l1_ln1_g, l1_ln1_b,
         (l1_ffn_w_up, l1_ffn_conv, l1_ffn_w_down), l1_ln2_g, l1_ln2_b),
        ((l2_mix_w_in, l2_mix_conv, l2_mix_w_out), l2_ln1_g, l2_ln1_b,
         (l2_ffn_w_up, l2_ffn_conv, l2_ffn_w_down), l2_ln2_g, l2_ln2_b),
        ((l3_mix_w_in, l3_mix_norm_g, l3_mix_w_out), l3_ln1_g, l3_ln1_b,
         (l3_ffn_w_up, l3_ffn_conv, l3_ffn_w_down), l3_ln2_g, l3_ln2_b),
    ]
    B, S, D = x.shape
    outs = []
    for bi in range(B):
        xf = x[bi]
        xb = xf.astype(BF16)
        for i in range(DEPTH):
            mix_p, ln1_g, ln1_b, ffn_p, ln2_g, ln2_b = layers[i]
            kind = i % N_MIXERS
            if kind == 0:
                w_in, norm_g, w_out = mix_p
                o = _hgrn(xb, _hgrn_lower_bound(hgrn_lower_bounds, i), w_in, norm_g)
                xf, xb = _ffn(xf, *ffn_p, ln2_g, ln2_b, mixer=(o, w_out, ln1_g, ln1_b))
            elif kind == 1:
                w_in, w_out = mix_p
                o = _moba(xb, positions[bi], w_in)
                xf, xb = _ffn(xf, *ffn_p, ln2_g, ln2_b, mixer=(o, w_out, ln1_g, ln1_b))
            else:
                w_in, conv_w, w_out = mix_p
                xf, xb = _conv_mixer(xb, xf, w_in, conv_w, w_out, ln1_g, ln1_b)
                xf, xb = _ffn(xf, *ffn_p, ln2_g, ln2_b, xb=xb)
        outs.append(xf)
    return outs[0][None] if B == 1 else jnp.stack(outs, axis=0)
```

```python
import functools
import math

import jax
import jax.numpy as jnp
from jax import lax
from jax.experimental import pallas as pl
from jax.experimental.pallas import tpu as pltpu

F32 = jnp.float32
BF16 = jnp.bfloat16

DEPTH = 4
N_MIXERS = 3
HGRN_HEADS = 8
HGRN_DK = 128
ATTN_HEADS = 8
HEAD_DIM = 128
MOBA_BLOCK = 256
MOBA_TOPK = 3
ROPE_THETA = 500000.0
ROPE_DIM = HEAD_DIM // 4
ROPE_HALF = ROPE_DIM // 2
LN_EPS = 1e-5
RMS_EPS = 1e-6
DEEPNORM_ALPHA = (2.0 * DEPTH) ** 0.25
MASK_NEG = -1e30

V7X_VMEM_BYTES = 64 * 1024 * 1024
VMEM_LIMIT_BYTES = V7X_VMEM_BYTES - 8 * 1024 * 1024
SUBLANES = 8
BF16_SUBLANES = 16
LANES = 128
MXU_DIM = 256

ROW_TILE = 512
FFN_ROW_TILE = 512
HGRN_CHUNK = 64
HGRN_SUB = SUBLANES
HGRN_MAX_CHUNK_DECAY = 80.0
HGRN_HEADS_PER_STEP = 4
ATTN_GROUP = 16
Q_SCALE = (1.0 / math.sqrt(HEAD_DIM)) * math.log2(math.e)


def _params(*sem):
    return pltpu.CompilerParams(dimension_semantics=sem, vmem_limit_bytes=VMEM_LIMIT_BYTES)


def _const_spec(shape):
    nd = len(shape)
    return pl.BlockSpec(shape, lambda *_: (0,) * nd, pipeline_mode=pl.Buffered(1))


def _sigmoid(x):
    return 1.0 / (1.0 + jnp.exp(-x))


def _layer_norm(y, g, b):
    mu = jnp.mean(y, axis=-1, keepdims=True)
    yc = y - mu
    var = jnp.mean(yc * yc, axis=-1, keepdims=True)
    return yc * lax.rsqrt(var + LN_EPS) * g + b


def _dot(a, b):
    return jnp.dot(a, b, preferred_element_type=F32)


def _dot_nt(a, b):
    return lax.dot_general(a, b, (((1,), (1,)), ((), ())), preferred_element_type=F32)


def _causal_conv3(u, w, carry_ref, col0):
    width = u.shape[1]
    cols = slice(col0, col0 + width)
    row = lax.broadcasted_iota(jnp.int32, (SUBLANES, width), 0)
    y = u * w[2:3]
    for j in (1, 2):
        r = pltpu.roll(u, j, 0)
        prev = carry_ref[j - 1, :, cols]
        carry_ref[j - 1, :, cols] = r[:SUBLANES]
        head = jnp.where(row < j, prev, r[:SUBLANES])
        y = y + jnp.concatenate([head, r[SUBLANES:]], axis=0) * w[2 - j:3 - j]
    return y


def _ffn_kernel(*refs, d_ff, tile, project):
    if project:
        (a_ref, wo_ref, g1_ref, b1_ref, xin_ref, wup_ref, cw_ref, wdn_ref, g_ref, b_ref,
         of_ref, ob_ref, carry_ref, h_ref, xf_ref) = refs
    else:
        (xb_ref, xf_ref, wup_ref, cw_ref, wdn_ref, g_ref, b_ref,
         of_ref, ob_ref, carry_ref, h_ref) = refs

    @pl.when(pl.program_id(0) == 0)
    def _():
        carry_ref[...] = jnp.zeros_like(carry_ref)

    if project:
        x = _layer_norm(DEEPNORM_ALPHA * xin_ref[...] + _dot(a_ref[...], wo_ref[...]),
                        g1_ref[...], b1_ref[...])
        xf_ref[...] = x
        xb = x.astype(BF16)
    else:
        xb = xb_ref[...]
    for c in range(d_ff // tile):
        ys = []
        for part in range(2):
            col0 = part * d_ff + c * tile
            u = _dot(xb, wup_ref[:, col0:col0 + tile])
            ys.append(_causal_conv3(u, cw_ref[:, col0:col0 + tile], carry_ref, col0))
        ya, yb = ys
        h_ref[:, c * tile:(c + 1) * tile] = (ya * _sigmoid(ya) * yb).astype(BF16)
    m = _dot(h_ref[...], wdn_ref[...])
    y = _layer_norm(DEEPNORM_ALPHA * xf_ref[...] + m, g_ref[...], b_ref[...])
    of_ref[...] = y
    ob_ref[...] = y.astype(BF16)


def _ffn(xf, w_up, conv_w, w_down, g, b, xb=None, mixer=None):
    S, D = xf.shape
    d_ff = w_down.shape[0]
    T = min(FFN_ROW_TILE, S)
    tile = MXU_DIM
    assert d_ff % tile == 0
    rows = lambda width: pl.BlockSpec((T, width), lambda i: (i, 0))
    vec = lambda v: v.reshape(1, D)
    ffn_specs = [_const_spec((D, 2 * d_ff)), _const_spec((3, 2 * d_ff)), _const_spec((d_ff, D)),
                 _const_spec((1, D)), _const_spec((1, D))]
    ffn_args = (w_up.astype(BF16), conv_w, w_down.astype(BF16), vec(g), vec(b))
    scratch = [pltpu.VMEM((2, SUBLANES, 2 * d_ff), F32), pltpu.VMEM((T, d_ff), BF16)]
    if mixer is None:
        in_specs = [rows(D), rows(D)] + ffn_specs
        args = (xb, xf) + ffn_args
    else:
        a, w_out, g1, b1 = mixer
        K = a.shape[1]
        in_specs = [rows(K), _const_spec((K, D)), _const_spec((1, D)), _const_spec((1, D)), rows(D)] + ffn_specs
        args = (a, w_out.astype(BF16), vec(g1), vec(b1), xf) + ffn_args
        scratch.append(pltpu.VMEM((T, D), F32))
    return pl.pallas_call(
        functools.partial(_ffn_kernel, d_ff=d_ff, tile=tile, project=mixer is not None),
        grid=(S // T,),
        in_specs=in_specs,
        out_specs=[rows(D), rows(D)],
        out_shape=[jax.ShapeDtypeStruct((S, D), F32), jax.ShapeDtypeStruct((S, D), BF16)],
        scratch_shapes=scratch,
        compiler_params=_params("arbitrary"),
        name="conv_ffn",
    )(*args)


def _conv_mixer_kernel(xb_ref, xf_ref, win_ref, cw_ref, wout_ref, g_ref, b_ref, of_ref, ob_ref,
                       carry_ref, y_ref, *, d_model, tile):
    @pl.when(pl.program_id(0) == 0)
    def _():
        carry_ref[...] = jnp.zeros_like(carry_ref)

    xb = xb_ref[...]
    for c in range(d_model // tile):
        col0 = c * tile
        bg = _dot(xb, win_ref[:, col0:col0 + tile])
        cg = _dot(xb, win_ref[:, d_model + col0:d_model + col0 + tile])
        hh = _dot(xb, win_ref[:, 2 * d_model + col0:2 * d_model + col0 + tile])
        conv = _causal_conv3(cg * hh, cw_ref[:, col0:col0 + tile], carry_ref, col0)
        y_ref[:, col0:col0 + tile] = (bg * conv).astype(BF16)
    m = _dot(y_ref[...], wout_ref[...])
    y = _layer_norm(DEEPNORM_ALPHA * xf_ref[...] + m, g_ref[...], b_ref[...])
    of_ref[...] = y
    ob_ref[...] = y.astype(BF16)


def _conv_mixer(xb, xf, w_in, conv_w, w_out, g, b):
    S, D = xf.shape
    T = min(ROW_TILE, S)
    return pl.pallas_call(
        functools.partial(_conv_mixer_kernel, d_model=D, tile=MXU_DIM),
        grid=(S // T,),
        in_specs=[
            pl.BlockSpec((T, D), lambda i: (i, 0)),
            pl.BlockSpec((T, D), lambda i: (i, 0)),
            _const_spec((D, 3 * D)),
            _const_spec((3, D)),
            _const_spec((D, D)),
            _const_spec((1, D)),
            _const_spec((1, D)),
        ],
        out_specs=[pl.BlockSpec((T, D), lambda i: (i, 0)), pl.BlockSpec((T, D), lambda i: (i, 0))],
        out_shape=[jax.ShapeDtypeStruct((S, D), F32), jax.ShapeDtypeStruct((S, D), BF16)],
        scratch_shapes=[
            pltpu.VMEM((2, SUBLANES, D), F32),
            pltpu.VMEM((T, D), BF16),
        ],
        compiler_params=_params("arbitrary"),
        name="conv_mixer",
    )(xb, xf, w_in.astype(BF16), conv_w, w_out.astype(BF16), g.reshape(1, D), b.reshape(1, D))


def _hgrn_prepare(xb, w, lb, rows):
    C, DK = HGRN_CHUNK, HGRN_DK
    nch = rows // C

    proj = _dot(xb, w)
    qz, fz, v, gz = (proj[:, j * DK:(j + 1) * DK] for j in range(4))
    q = qz * _sigmoid(qz)
    sig = _sigmoid(fz)
    k = (1.0 - lb) * (1.0 - sig)
    g = jnp.log(lb + (1.0 - lb) * sig)

    def to_lanes(a):
        return jnp.concatenate([a[c * C:(c + 1) * C] for c in range(nch)], axis=1)

    q2, k2, g2, v2 = to_lanes(q), to_lanes(k), to_lanes(g), to_lanes(v)

    ri = lax.broadcasted_iota(jnp.int32, (C, C), 0)
    ci = lax.broadcasted_iota(jnp.int32, (C, C), 1)
    tri = (ri >= ci).astype(BF16)
    g_hi = g2.astype(BF16)
    g_rem = g2 - g_hi.astype(F32)
    g_mid = g_rem.astype(BF16)
    g_lo = (g_rem - g_mid.astype(F32)).astype(BF16)
    b2 = _dot(tri, g_hi) + _dot(tri, g_mid) + _dot(tri, g_lo)
    return dict(q2=q2, k2=k2, v2=v2, b2=b2, gz=gz)


def _hgrn_chunk_decay_columns(b2):
    C, DK = HGRN_CHUNK, HGRN_DK
    nch = b2.shape[1] // DK
    b_last = b2[C - 1:C]
    rows = [b_last[:, c * DK:(c + 1) * DK] for c in range(nch)]
    padded = jnp.concatenate(rows + [jnp.zeros((DK - nch, DK), F32)], axis=0)
    return jnp.exp(padded.T)


def _hgrn_chunks_factored(p, st):
    C, DK = HGRN_CHUNK, HGRN_DK
    q2, k2, v2, b2 = p["q2"], p["k2"], p["v2"], p["b2"]
    qe = (q2 * jnp.exp(b2)).astype(BF16)
    kinv = k2 * jnp.exp(-b2)
    kinv_b = kinv.astype(BF16)
    v_b = v2.astype(BF16)
    causal = lax.broadcasted_iota(jnp.int32, (C, C), 0) >= lax.broadcasted_iota(jnp.int32, (C, C), 1)
    decay = _hgrn_chunk_decay_columns(b2)
    outs = []
    for c in range(q2.shape[1] // DK):
        sl = slice(c * DK, (c + 1) * DK)
        o_inter = _dot(qe[:, sl], st.astype(BF16))
        a = jnp.where(causal, _dot_nt(qe[:, sl], kinv_b[:, sl]), 0.0).astype(BF16)
        outs.append(o_inter + _dot(a, v_b[:, sl]))
        st = decay[:, c:c + 1] * (st + _dot(kinv[:, sl].T.astype(BF16), v_b[:, sl]))
    return jnp.concatenate(outs, axis=0), st


def _hgrn_chunks_robust(p, st):
    C, SUB, DK = HGRN_CHUNK, HGRN_SUB, HGRN_DK
    nsub = C // SUB
    q2, k2, v2, b2 = p["q2"], p["k2"], p["v2"], p["b2"]
    width = q2.shape[1]
    b_last = b2[C - 1:C]
    qe = (q2 * jnp.exp(b2)).astype(BF16)
    kdec = k2 * jnp.exp(b_last - b2)
    v_b = v2.astype(BF16)
    refs = [jnp.broadcast_to(b2[SUB * i - 1:SUB * i], (SUB, width)) for i in range(1, nsub)]
    rb = jnp.concatenate([jnp.zeros((SUB, width), F32)] + refs, axis=0)
    qp = (q2 * jnp.exp(b2 - rb)).astype(BF16)
    kst, vst = [], []
    for i in range(1, nsub):
        ref_i = jnp.concatenate([refs[i - 1]] * i, axis=0)
        kst.append(k2[:SUB * i] * jnp.exp(ref_i - b2[:SUB * i]))
        vst.append(v2[:SUB * i])
    kst = jnp.concatenate(kst, axis=0).astype(BF16)
    vst = jnp.concatenate(vst, axis=0).astype(BF16)
    nstack = kst.shape[0]
    row_blk = lax.broadcasted_iota(jnp.int32, (C, nstack), 0) // SUB
    col = lax.broadcasted_iota(jnp.int32, (C, nstack), 1)
    col_blk = jnp.ones((C, nstack), jnp.int32)
    for i in range(2, nsub):
        col_blk = col_blk + (col >= (SUB * i * (i - 1)) // 2).astype(jnp.int32)
    stack_mask = row_blk == col_blk
    o_diag = _hgrn_exact_diagonal(p)
    decay = _hgrn_chunk_decay_columns(b2)
    outs = []
    for c in range(width // DK):
        sl = slice(c * DK, (c + 1) * DK)
        r = jnp.where(stack_mask, _dot_nt(qp[:, sl], kst[:, sl]), 0.0).astype(BF16)
        outs.append(_dot(qe[:, sl], st.astype(BF16)) + _dot(r, vst[:, sl]) + o_diag[:, sl])
        st = decay[:, c:c + 1] * st + _dot(kdec[:, sl].T.astype(BF16), v_b[:, sl])
    return jnp.concatenate(outs, axis=0), st


def _hgrn_exact_diagonal(p):
    C, SUB, DK = HGRN_CHUNK, HGRN_SUB, HGRN_DK
    q2, k2, v2, b2 = p["q2"], p["k2"], p["v2"], p["b2"]
    width = q2.shape[1]
    trow = lax.broadcasted_iota(jnp.int32, (C, width), 0) % SUB
    ps, vds = [q2 * k2], [v2]
    for d in range(1, SUB):
        valid = trow >= d
        kd = pltpu.roll(k2, d, 0)
        bd = pltpu.roll(b2, d, 0)
        vds.append(pltpu.roll(v2, d, 0))
        e = jnp.where(valid, b2 - bd, 0.0)
        ps.append(jnp.where(valid, q2 * kd * jnp.exp(e), 0.0))
    li = lax.broadcasted_iota(jnp.int32, (MXU_DIM, MXU_DIM), 0) // DK
    lj = lax.broadcasted_iota(jnp.int32, (MXU_DIM, MXU_DIM), 1) // DK
    group_ones = (li == lj).astype(BF16)
    out = []
    for p0 in range(0, width, MXU_DIM):
        lhs = jnp.concatenate([x[:, p0:p0 + MXU_DIM] for x in ps], axis=0).astype(BF16)
        a = _dot(lhs, group_ones)
        acc = a[:C] * vds[0][:, p0:p0 + MXU_DIM]
        for d in range(1, SUB):
            acc = acc + a[d * C:(d + 1) * C] * vds[d][:, p0:p0 + MXU_DIM]
        out.append(acc)
    return jnp.concatenate(out, axis=1)


def _hgrn_kernel(xb_ref, w_ref, lb_ref, ng_ref, o_ref, st_ref, *, rows, heads):
    DK = HGRN_DK

    @pl.when(pl.program_id(1) == 0)
    def _():
        st_ref[...] = jnp.zeros_like(st_ref)

    xb = xb_ref[...]
    prep = [_hgrn_prepare(xb, w_ref[h], lb_ref[h], rows) for h in range(heads)]
    in_range = jnp.min(prep[0]["b2"]) >= -HGRN_MAX_CHUNK_DECAY
    for p in prep[1:]:
        in_range = jnp.logical_and(in_range, jnp.min(p["b2"]) >= -HGRN_MAX_CHUNK_DECAY)

    def run(chunks_fn):
        for h, p in enumerate(prep):
            o, st = chunks_fn(p, st_ref[h])
            st_ref[h] = st
            o = o * lax.rsqrt(jnp.mean(o * o, axis=-1, keepdims=True) + RMS_EPS)
            gz = p["gz"]
            o_ref[:, h * DK:(h + 1) * DK] = (o * ng_ref[h] * (gz * _sigmoid(gz))).astype(BF16)

    @pl.when(in_range)
    def _():
        run(_hgrn_chunks_factored)

    @pl.when(jnp.logical_not(in_range))
    def _():
        run(_hgrn_chunks_robust)


def _hgrn(xb, lb, w_in, norm_g):
    S, D = xb.shape
    H, DK, HP = HGRN_HEADS, HGRN_DK, HGRN_HEADS_PER_STEP
    T = min(ROW_TILE, S)
    w = w_in.astype(BF16).reshape(D, 4, H, DK).transpose(2, 0, 1, 3).reshape(H, D, 4 * DK)
    return pl.pallas_call(
        functools.partial(_hgrn_kernel, rows=T, heads=HP),
        grid=(H // HP, S // T),
        in_specs=[
            pl.BlockSpec((T, D), lambda h, t: (t, 0)),
            pl.BlockSpec((HP, D, 4 * DK), lambda h, t: (h, 0, 0)),
            pl.BlockSpec((HP, 1, DK), lambda h, t: (h, 0, 0)),
            pl.BlockSpec((HP, 1, DK), lambda h, t: (h, 0, 0)),
        ],
        out_specs=pl.BlockSpec((T, HP * DK), lambda h, t: (t, h)),
        out_shape=jax.ShapeDtypeStruct((S, H * DK), BF16),
        scratch_shapes=[pltpu.VMEM((HP, DK, DK), F32)],
        compiler_params=_params("arbitrary", "arbitrary"),
        name="hgrn2",
    )(xb, w, lb.reshape(H, 1, DK), norm_g.reshape(H, 1, DK))


def _qkv_kernel(xb_ref, posr_ref, posc_ref, wqt_ref, wk_ref, wvt_ref, fcol_ref, frow_ref,
                qt_ref, k_ref, vt_ref, *, rows):
    H, HD, L = ATTN_HEADS, HEAD_DIM, MOBA_BLOCK
    x = xb_ref[...]
    qt = _dot_nt(wqt_ref[...], x)
    vt = _dot_nt(wvt_ref[...], x)
    k = _dot(x, wk_ref[...])

    ang_t = fcol_ref[...] * posr_ref[...].astype(F32)
    cos_t, sin_t = jnp.cos(ang_t), jnp.sin(ang_t)
    ang = posc_ref[...].astype(F32) * frow_ref[...]
    cos_r, sin_r = jnp.cos(ang), jnp.sin(ang)
    lane = lax.broadcasted_iota(jnp.int32, (rows, HD), 1)
    first = lane < ROPE_HALF
    sin_r = jnp.where(first, -sin_r, sin_r)

    for h in range(H):
        blk = qt[h * HD:(h + 1) * HD]
        x1, x2 = blk[:ROPE_HALF], blk[ROPE_HALF:ROPE_DIM]
        qh = jnp.concatenate([x1 * cos_t - x2 * sin_t, x2 * cos_t + x1 * sin_t, blk[ROPE_DIM:]], axis=0)
        qh = (qh * Q_SCALE).astype(BF16)
        kh = k[:, h * HD:(h + 1) * HD]
        partner = jnp.where(first, pltpu.roll(kh, HD - ROPE_HALF, 1), pltpu.roll(kh, ROPE_HALF, 1))
        kh = (kh * cos_r + partner * sin_r).astype(BF16)
        vh = vt[h * HD:(h + 1) * HD].astype(BF16)
        for n in range(rows // L):
            qt_ref[h, n] = qh[:, n * L:(n + 1) * L]
            vt_ref[h, n] = vh[:, n * L:(n + 1) * L]
            k_ref[h, n] = kh[n * L:(n + 1) * L]


def _attn_kernel(qt_ref, k_ref, vt_ref, o_ref, kmean_ref, sel_ref, s_ref, gmax_ref, m_ref, acc_ref, *, nblk):
    L = MOBA_BLOCK
    i = pl.program_id(1)

    G = ATTN_GROUP

    @pl.when(i == 0)
    def _():
        def mean_body(n, carry):
            kmean_ref[pl.ds(n, 1), :] = jnp.mean(k_ref[n].astype(F32), axis=0, keepdims=True)
            return carry
        lax.fori_loop(0, nblk, mean_body, 0)

    qt = qt_ref[...]
    km = kmean_ref[...]
    km_hi = km.astype(BF16)
    km_lo = (km - km_hi.astype(F32)).astype(BF16)
    gate = _dot(km_hi, qt) + _dot(km_lo, qt)

    kpos = lax.broadcasted_iota(jnp.int32, (L, L), 0)
    qpos = lax.broadcasted_iota(jnp.int32, (L, L), 1)
    s_own = jnp.where(kpos <= qpos, _dot(k_ref[i], qt), MASK_NEG)

    rowj = lax.broadcasted_iota(jnp.int32, (nblk, L), 0)
    gate = jnp.where(rowj < i, gate, -jnp.inf)
    sel = jnp.zeros((nblk, L), F32)
    for kk in range(MOBA_TOPK):
        mx = jnp.max(gate, axis=0, keepdims=True)
        idx = jnp.min(jnp.where(gate == mx, rowj, nblk), axis=0, keepdims=True)
        pick = jnp.logical_and(rowj == idx, kk < i)
        sel = jnp.where(pick, 1.0, sel)
        gate = jnp.where(pick, -jnp.inf, gate)
    sel_ref[:nblk] = sel

    def group_max(s):
        return jnp.max(s.reshape(L // SUBLANES, SUBLANES, L), axis=0)

    def with_ones(vt):
        return jnp.concatenate([vt, jnp.ones((BF16_SUBLANES, vt.shape[1]), BF16)], axis=0)

    m0 = jnp.max(s_own, axis=0, keepdims=True)
    m_ref[...] = m0
    acc_ref[...] = _dot(with_ones(vt_ref[i]), jnp.exp2(s_own - m0).astype(BF16))

    past = nblk - 1
    ngroups = pl.cdiv(past, G)
    ng = (i + G - 1) // G

    def stage_scores(t):
        buf, m8 = t % 2, None
        for u in range(min(G, past - t * G)):
            jj = t * G + u
            s = jnp.where(sel_ref[jj:jj + 1, :] > 0.5, _dot(k_ref[jj], qt), MASK_NEG)
            s_ref[buf, u] = s
            m8 = group_max(s) if m8 is None else jnp.maximum(m8, group_max(s))
        gmax_ref[buf] = m8

    def stage_accumulate(t):
        buf = t % 2
        m_old = m_ref[...]
        m_new = jnp.maximum(m_old, jnp.max(gmax_ref[buf], axis=0, keepdims=True))
        total = None
        for u in range(min(G, past - t * G)):
            p = jnp.exp2(s_ref[buf, u] - m_new).astype(BF16)
            part = _dot(with_ones(vt_ref[t * G + u]), p)
            total = part if total is None else total + part
        acc_ref[...] = jnp.exp2(m_old - m_new) * acc_ref[...] + total
        m_ref[...] = m_new

    if ngroups > 0:
        stage_scores(0)
    for t in range(ngroups):
        if t + 1 < ngroups:
            @pl.when(t + 1 < ng)
            def _():
                stage_accumulate(t)
                stage_scores(t + 1)

        @pl.when(t + 1 == ng)
        def _():
            stage_accumulate(t)

    acc = acc_ref[...]
    o_ref[...] = (acc[:HEAD_DIM] / acc[HEAD_DIM:HEAD_DIM + 1]).T.astype(BF16)


def _moba(xb, positions, w_in):
    S, D = xb.shape
    H, HD, L = ATTN_HEADS, HEAD_DIM, MOBA_BLOCK
    assert S % L == 0
    NB = S // L
    T = min(ROW_TILE, S)
    nper = T // L
    wb = w_in.astype(BF16)
    wqt, wk, wvt = wb[:, :D].T, wb[:, D:2 * D], wb[:, 2 * D:].T
    inv_freq = 1.0 / (ROPE_THETA ** (jnp.arange(0, ROPE_DIM, 2, dtype=F32) / ROPE_DIM))
    fcol = inv_freq.reshape(ROPE_HALF, 1)
    frow = jnp.concatenate([inv_freq, inv_freq, jnp.zeros((HD - ROPE_DIM,), F32)]).reshape(1, HD)
    pos = positions.reshape(S).astype(jnp.int32)
    qt, k, vt = pl.pallas_call(
        functools.partial(_qkv_kernel, rows=T),
        grid=(S // T,),
        in_specs=[
            pl.BlockSpec((T, D), lambda i: (i, 0)),
            pl.BlockSpec((1, T), lambda i: (0, i)),
            pl.BlockSpec((T, 1), lambda i: (i, 0)),
            _const_spec((D, D)),
            _const_spec((D, D)),
            _const_spec((D, D)),
            _const_spec((ROPE_HALF, 1)),
            _const_spec((1, HD)),
        ],
        out_specs=[
            pl.BlockSpec((H, nper, HD, L), lambda i: (0, i, 0, 0)),
            pl.BlockSpec((H, nper, L, HD), lambda i: (0, i, 0, 0)),
            pl.BlockSpec((H, nper, HD, L), lambda i: (0, i, 0, 0)),
        ],
        out_shape=[
            jax.ShapeDtypeStruct((H, NB, HD, L), BF16),
            jax.ShapeDtypeStruct((H, NB, L, HD), BF16),
            jax.ShapeDtypeStruct((H, NB, HD, L), BF16),
        ],
        compiler_params=_params("arbitrary"),
        name="moba_qkv_rope",
    )(xb, pos.reshape(1, S), pos.reshape(S, 1), wqt, wk, wvt, fcol, frow)

    return pl.pallas_call(
        functools.partial(_attn_kernel, nblk=NB),
        grid=(H, NB),
        in_specs=[
            pl.BlockSpec((None, None, HD, L), lambda h, i: (h, i, 0, 0)),
            pl.BlockSpec((None, NB, L, HD), lambda h, i: (h, 0, 0, 0)),
            pl.BlockSpec((None, NB, HD, L), lambda h, i: (h, 0, 0, 0)),
        ],
        out_specs=pl.BlockSpec((L, HD), lambda h, i: (i, h)),
        out_shape=jax.ShapeDtypeStruct((S, H * HD), BF16),
        scratch_shapes=[
            pltpu.VMEM((NB, HD), F32),
            pltpu.VMEM((NB, L), F32),
            pltpu.VMEM((2, ATTN_GROUP, L, L), F32),
            pltpu.VMEM((2, SUBLANES, L), F32),
            pltpu.VMEM((1, L), F32),
            pltpu.VMEM((HD + BF16_SUBLANES, L), F32),
        ],
        compiler_params=_params("arbitrary", "arbitrary"),
        name="moba_attention",
    )(qt, k, vt)


def _hgrn_lower_bound(lower_bounds, layer):
    s = jax.nn.softmax(lower_bounds.astype(F32), axis=0)
    return (jnp.cumsum(s, axis=0) - s[0])[layer]


def kernel(x, positions, hgrn_lower_bounds, l0_mix_w_in, l0_mix_norm_g, l0_mix_w_out, l0_ln1_g, l0_ln1_b, l0_ffn_w_up, l0_ffn_conv, l0_ffn_w_down, l0_ln2_g, l0_ln2_b, l1_mix_w_in, l1_mix_w_out, l1_ln1_g, l1_ln1_b, l1_ffn_w_up, l1_ffn_conv, l1_ffn_w_down, l1_ln2_g, l1_ln2_b, l2_mix_w_in, l2_mix_conv, l2_mix_w_out, l2_ln1_g, l2_ln1_b, l2_ffn_w_up, l2_ffn_conv, l2_ffn_w_down, l2_ln2_g, l2_ln2_b, l3_mix_w_in, l3_mix_norm_g, l3_mix_w_out, l3_ln1_g, l3_ln1_b, l3_ffn_w_up, l3_ffn_conv, l3_ffn_w_down, l3_ln2_g, l3_ln2_b):
    layers = [
        ((l0_mix_w_in, l0_mix_norm_g, l0_mix_w_out), l0_ln1_g, l0_ln1_b,
         (l0_ffn_w_up, l0_ffn_conv, l0_ffn_w_down), l0_ln2_g, l0_ln2_b),
        ((l1_mix_w_in, l1_mix_w_out), l1_ln1_g, l1_ln1_b,
         (l1_ffn_w_up, l1_ffn_conv, l1_ffn_w_down), l1_ln2_g, l1_ln2_b),
        ((l2_mix_w_in, l2_mix_conv, l2_mix_w_out), l2_ln1_g, l2_ln1_b,
         (l2_ffn_w_up, l2_ffn_conv, l2_ffn_w_down), l2_ln2_g, l2_ln2_b),
        ((l3_mix_w_in, l3_mix_norm_g, l3_mix_w_out), l3_ln1_g, l3_ln1_b,
         (l3_ffn_w_up, l3_ffn_conv, l3_ffn_w_down), l3_ln2_g, l3_ln2_b),
    ]
    B, S, D = x.shape
    outs = []
    for bi in range(B):
        xf = x[bi]
        xb = xf.astype(BF16)
        for i in range(DEPTH):
            mix_p, ln1_g, ln1_b, ffn_p, ln2_g, ln2_b = layers[i]
            kind = i % N_MIXERS
            if kind == 0:
                w_in, norm_g, w_out = mix_p
                o = _hgrn(xb, _hgrn_lower_bound(hgrn_lower_bounds, i), w_in, norm_g)
                xf, xb = _ffn(xf, *ffn_p, ln2_g, ln2_b, mixer=(o, w_out, ln1_g, ln1_b))
            elif kind == 1:
                w_in, w_out = mix_p
                o = _moba(xb, positions[bi], w_in)
                xf, xb = _ffn(xf, *ffn_p, ln2_g, ln2_b, mixer=(o, w_out, ln1_g, ln1_b))
            else:
                w_in, conv_w, w_out = mix_p
                xf, xb = _conv_mixer(xb, xf, w_in, conv_w, w_out, ln1_g, ln1_b)
                xf, xb = _ffn(xf, *ffn_p, ln2_g, ln2_b, xb=xb)
        outs.append(xf)
    return outs[0][None] if B == 1 else jnp.stack(outs, axis=0)
```

```python
import functools
import math

import jax
import jax.numpy as jnp
from jax import lax
from jax.experimental import pallas as pl
from jax.experimental.pallas import tpu as pltpu

F32 = jnp.float32
BF16 = jnp.bfloat16

DEPTH = 4
N_MIXERS = 3
HGRN_HEADS = 8
HGRN_DK = 128
ATTN_HEADS = 8
HEAD_DIM = 128
MOBA_BLOCK = 256
MOBA_TOPK = 3
ROPE_THETA = 500000.0
ROPE_DIM = HEAD_DIM // 4
ROPE_HALF = ROPE_DIM // 2
LN_EPS = 1e-5
RMS_EPS = 1e-6
DEEPNORM_ALPHA = (2.0 * DEPTH) ** 0.25
MASK_NEG = -1e30

V7X_VMEM_BYTES = 64 * 1024 * 1024
VMEM_LIMIT_BYTES = V7X_VMEM_BYTES - 8 * 1024 * 1024
SUBLANES = 8
BF16_SUBLANES = 16
LANES = 128
MXU_DIM = 256

ROW_TILE = 512
FFN_ROW_TILE = 512
HGRN_CHUNK = 64
HGRN_SUB = SUBLANES
HGRN_MAX_CHUNK_DECAY = 80.0
HGRN_HEADS_PER_STEP = 4
ATTN_GROUP = 16
Q_SCALE = (1.0 / math.sqrt(HEAD_DIM)) * math.log2(math.e)


def _params(*sem):
    return pltpu.CompilerParams(dimension_semantics=sem, vmem_limit_bytes=VMEM_LIMIT_BYTES)


def _const_spec(shape):
    nd = len(shape)
    return pl.BlockSpec(shape, lambda *_: (0,) * nd, pipeline_mode=pl.Buffered(1))


def _sigmoid(x):
    return 1.0 / (1.0 + jnp.exp(-x))


def _layer_norm(y, g, b):
    mu = jnp.mean(y, axis=-1, keepdims=True)
    yc = y - mu
    var = jnp.mean(yc * yc, axis=-1, keepdims=True)
    return yc * lax.rsqrt(var + LN_EPS) * g + b


def _dot(a, b):
    return jnp.dot(a, b, preferred_element_type=F32)


def _dot_nt(a, b):
    return lax.dot_general(a, b, (((1,), (1,)), ((), ())), preferred_element_type=F32)


def _causal_conv3(u, w, carry_ref, col0):
    width = u.shape[1]
    cols = slice(col0, col0 + width)
    row = lax.broadcasted_iota(jnp.int32, (SUBLANES, width), 0)
    y = u * w[2:3]
    for j in (1, 2):
        r = pltpu.roll(u, j, 0)
        prev = carry_ref[j - 1, :, cols]
        carry_ref[j - 1, :, cols] = r[:SUBLANES]
        head = jnp.where(row < j, prev, r[:SUBLANES])
        y = y + jnp.concatenate([head, r[SUBLANES:]], axis=0) * w[2 - j:3 - j]
    return y


def _ffn_kernel(*refs, d_ff, tile, project):
    if project:
        (a_ref, wo_ref, g1_ref, b1_ref, xin_ref, wup_ref, cw_ref, wdn_ref, g_ref, b_ref,
         of_ref, ob_ref, carry_ref, h_ref, xf_ref) = refs
    else:
        (xb_ref, xf_ref, wup_ref, cw_ref, wdn_ref, g_ref, b_ref,
         of_ref, ob_ref, carry_ref, h_ref) = refs

    @pl.when(pl.program_id(0) == 0)
    def _():
        carry_ref[...] = jnp.zeros_like(carry_ref)

    if project:
        x = _layer_norm(DEEPNORM_ALPHA * xin_ref[...] + _dot(a_ref[...], wo_ref[...]),
                        g1_ref[...], b1_ref[...])
        xf_ref[...] = x
        xb = x.astype(BF16)
    else:
        xb = xb_ref[...]
    for c in range(d_ff // tile):
        ys = []
        for part in range(2):
            col0 = part * d_ff + c * tile
            u = _dot(xb, wup_ref[:, col0:col0 + tile])
            ys.append(_causal_conv3(u, cw_ref[:, col0:col0 + tile], carry_ref, col0))
        ya, yb = ys
        h_ref[:, c * tile:(c + 1) * tile] = (ya * _sigmoid(ya) * yb).astype(BF16)
    m = _dot(h_ref[...], wdn_ref[...])
    y = _layer_norm(DEEPNORM_ALPHA * xf_ref[...] + m, g_ref[...], b_ref[...])
    of_ref[...] = y
    ob_ref[...] = y.astype(BF16)


def _ffn(xf, w_up, conv_w, w_down, g, b, xb=None, mixer=None):
    S, D = xf.shape
    d_ff = w_down.shape[0]
    T = min(FFN_ROW_TILE, S)
    tile = MXU_DIM
    assert d_ff % tile == 0
    rows = lambda width: pl.BlockSpec((T, width), lambda i: (i, 0))
    vec = lambda v: v.reshape(1, D)
    ffn_specs = [_const_spec((D, 2 * d_ff)), _const_spec((3, 2 * d_ff)), _const_spec((d_ff, D)),
                 _const_spec((1, D)), _const_spec((1, D))]
    ffn_args = (w_up.astype(BF16), conv_w, w_down.astype(BF16), vec(g), vec(b))
    scratch = [pltpu.VMEM((2, SUBLANES, 2 * d_ff), F32), pltpu.VMEM((T, d_ff), BF16)]
    if mixer is None:
        in_specs = [rows(D), rows(D)] + ffn_specs
        args = (xb, xf) + ffn_args
    else:
        a, w_out, g1, b1 = mixer
        K = a.shape[1]
        in_specs = [rows(K), _const_spec((K, D)), _const_spec((1, D)), _const_spec((1, D)), rows(D)] + ffn_specs
        args = (a, w_out.astype(BF16), vec(g1), vec(b1), xf) + ffn_args
        scratch.append(pltpu.VMEM((T, D), F32))
    return pl.pallas_call(
        functools.partial(_ffn_kernel, d_ff=d_ff, tile=tile, project=mixer is not None),
        grid=(S // T,),
        in_specs=in_specs,
        out_specs=[rows(D), rows(D)],
        out_shape=[jax.ShapeDtypeStruct((S, D), F32), jax.ShapeDtypeStruct((S, D), BF16)],
        scratch_shapes=scratch,
        compiler_params=_params("arbitrary"),
        name="conv_ffn",
    )(*args)


def _conv_mixer_kernel(xb_ref, xf_ref, win_ref, cw_ref, wout_ref, g_ref, b_ref, of_ref, ob_ref,
                       carry_ref, y_ref, *, d_model, tile):
    @pl.when(pl.program_id(0) == 0)
    def _():
        carry_ref[...] = jnp.zeros_like(carry_ref)

    xb = xb_ref[...]
    for c in range(d_model // tile):
        col0 = c * tile
        bg = _dot(xb, win_ref[:, col0:col0 + tile])
        cg = _dot(xb, win_ref[:, d_model + col0:d_model + col0 + tile])
        hh = _dot(xb, win_ref[:, 2 * d_model + col0:2 * d_model + col0 + tile])
        conv = _causal_conv3(cg * hh, cw_ref[:, col0:col0 + tile], carry_ref, col0)
        y_ref[:, col0:col0 + tile] = (bg * conv).astype(BF16)
    m = _dot(y_ref[...], wout_ref[...])
    y = _layer_norm(DEEPNORM_ALPHA * xf_ref[...] + m, g_ref[...], b_ref[...])
    of_ref[...] = y
    ob_ref[...] = y.astype(BF16)


def _conv_mixer(xb, xf, w_in, conv_w, w_out, g, b):
    S, D = xf.shape
    T = min(ROW_TILE, S)
    return pl.pallas_call(
        functools.partial(_conv_mixer_kernel, d_model=D, tile=MXU_DIM),
        grid=(S // T,),
        in_specs=[
            pl.BlockSpec((T, D), lambda i: (i, 0)),
            pl.BlockSpec((T, D), lambda i: (i, 0)),
            _const_spec((D, 3 * D)),
            _const_spec((3, D)),
            _const_spec((D, D)),
            _const_spec((1, D)),
            _const_spec((1, D)),
        ],
        out_specs=[pl.BlockSpec((T, D), lambda i: (i, 0)), pl.BlockSpec((T, D), lambda i: (i, 0))],
        out_shape=[jax.ShapeDtypeStruct((S, D), F32), jax.ShapeDtypeStruct((S, D), BF16)],
        scratch_shapes=[
            pltpu.VMEM((2, SUBLANES, D), F32),
            pltpu.VMEM((T, D), BF16),
        ],
        compiler_params=_params("arbitrary"),
        name="conv_mixer",
    )(xb, xf, w_in.astype(BF16), conv_w, w_out.astype(BF16), g.reshape(1, D), b.reshape(1, D))


def _hgrn_prepare(xb, w, lb, rows):
    C, DK = HGRN_CHUNK, HGRN_DK
    nch = rows // C

    proj = _dot(xb, w)
    qz, fz, v, gz = (proj[:, j * DK:(j + 1) * DK] for j in range(4))
    q = qz * _sigmoid(qz)
    sig = _sigmoid(fz)
    k = (1.0 - lb) * (1.0 - sig)
    g = jnp.log(lb + (1.0 - lb) * sig)

    def to_lanes(a):
        return jnp.concatenate([a[c * C:(c + 1) * C] for c in range(nch)], axis=1)

    q2, k2, g2, v2 = to_lanes(q), to_lanes(k), to_lanes(g), to_lanes(v)

    ri = lax.broadcasted_iota(jnp.int32, (C, C), 0)
    ci = lax.broadcasted_iota(jnp.int32, (C, C), 1)
    tri = (ri >= ci).astype(BF16)
    g_hi = g2.astype(BF16)
    g_rem = g2 - g_hi.astype(F32)
    g_mid = g_rem.astype(BF16)
    g_lo = (g_rem - g_mid.astype(F32)).astype(BF16)
    b2 = _dot(tri, g_hi) + _dot(tri, g_mid) + _dot(tri, g_lo)
    return dict(q2=q2, k2=k2, v2=v2, b2=b2, gz=gz)


def _hgrn_chunk_decay_columns(b2):
    C, DK = HGRN_CHUNK, HGRN_DK
    nch = b2.shape[1] // DK
    b_last = b2[C - 1:C]
    rows = [b_last[:, c * DK:(c + 1) * DK] for c in range(nch)]
    padded = jnp.concatenate(rows + [jnp.zeros((DK - nch, DK), F32)], axis=0)
    return jnp.exp(padded.T)


def _hgrn_chunks_factored(p, st):
    C, DK = HGRN_CHUNK, HGRN_DK
    q2, k2, v2, b2 = p["q2"], p["k2"], p["v2"], p["b2"]
    qe = (q2 * jnp.exp(b2)).astype(BF16)
    kinv = k2 * jnp.exp(-b2)
    kinv_b = kinv.astype(BF16)
    v_b = v2.astype(BF16)
    causal = lax.broadcasted_iota(jnp.int32, (C, C), 0) >= lax.broadcasted_iota(jnp.int32, (C, C), 1)
    decay = _hgrn_chunk_decay_columns(b2)
    outs = []
    for c in range(q2.shape[1] // DK):
        sl = slice(c * DK, (c + 1) * DK)
        o_inter = _dot(qe[:, sl], st.astype(BF16))
        a = jnp.where(causal, _dot_nt(qe[:, sl], kinv_b[:, sl]), 0.0).astype(BF16)
        outs.append(o_inter + _dot(a, v_b[:, sl]))
        st = decay[:, c:c + 1] * (st + _dot(kinv[:, sl].T.astype(BF16), v_b[:, sl]))
    return jnp.concatenate(outs, axis=0), st


def _hgrn_chunks_robust(p, st):
    C, SUB, DK = HGRN_CHUNK, HGRN_SUB, HGRN_DK
    nsub = C // SUB
    q2, k2, v2, b2 = p["q2"], p["k2"], p["v2"], p["b2"]
    width = q2.shape[1]
    b_last = b2[C - 1:C]
    qe = (q2 * jnp.exp(b2)).astype(BF16)
    kdec = k2 * jnp.exp(b_last - b2)
    v_b = v2.astype(BF16)
    refs = [jnp.broadcast_to(b2[SUB * i - 1:SUB * i], (SUB, width)) for i in range(1, nsub)]
    rb = jnp.concatenate([jnp.zeros((SUB, width), F32)] + refs, axis=0)
    qp = (q2 * jnp.exp(b2 - rb)).astype(BF16)
    kst, vst = [], []
    for i in range(1, nsub):
        ref_i = jnp.concatenate([refs[i - 1]] * i, axis=0)
        kst.append(k2[:SUB * i] * jnp.exp(ref_i - b2[:SUB * i]))
        vst.append(v2[:SUB * i])
    kst = jnp.concatenate(kst, axis=0).astype(BF16)
    vst = jnp.concatenate(vst, axis=0).astype(BF16)
    nstack = kst.shape[0]
    row_blk = lax.broadcasted_iota(jnp.int32, (C, nstack), 0) // SUB
    col = lax.broadcasted_iota(jnp.int32, (C, nstack), 1)
    col_blk = jnp.ones((C, nstack), jnp.int32)
    for i in range(2, nsub):
        col_blk = col_blk + (col >= (SUB * i * (i - 1)) // 2).astype(jnp.int32)
    stack_mask = row_blk == col_blk
    o_diag = _hgrn_exact_diagonal(p)
    decay = _hgrn_chunk_decay_columns(b2)
    outs = []
    for c in range(width // DK):
        sl = slice(c * DK, (c + 1) * DK)
        r = jnp.where(stack_mask, _dot_nt(qp[:, sl], kst[:, sl]), 0.0).astype(BF16)
        outs.append(_dot(qe[:, sl], st.astype(BF16)) + _dot(r, vst[:, sl]) + o_diag[:, sl])
        st = decay[:, c:c + 1] * st + _dot(kdec[:, sl].T.astype(BF16), v_b[:, sl])
    return jnp.concatenate(outs, axis=0), st


def _hgrn_exact_diagonal(p):
    C, SUB, DK = HGRN_CHUNK, HGRN_SUB, HGRN_DK
    q2, k2, v2, b2 = p["q2"], p["k2"], p["v2"], p["b2"]
    width = q2.shape[1]
    trow = lax.broadcasted_iota(jnp.int32, (C, width), 0) % SUB
    ps, vds = [q2 * k2], [v2]
    for d in range(1, SUB):
        valid = trow >= d
        kd = pltpu.roll(k2, d, 0)
        bd = pltpu.roll(b2, d, 0)
        vds.append(pltpu.roll(v2, d, 0))
        e = jnp.where(valid, b2 - bd, 0.0)
        ps.append(jnp.where(valid, q2 * kd * jnp.exp(e), 0.0))
    li = lax.broadcasted_iota(jnp.int32, (MXU_DIM, MXU_DIM), 0) // DK
    lj = lax.broadcasted_iota(jnp.int32, (MXU_DIM, MXU_DIM), 1) // DK
    group_ones = (li == lj).astype(BF16)
    out = []
    for p0 in range(0, width, MXU_DIM):
        lhs = jnp.concatenate([x[:, p0:p0 + MXU_DIM] for x in ps], axis=0).astype(BF16)
        a = _dot(lhs, group_ones)
        acc = a[:C] * vds[0][:, p0:p0 + MXU_DIM]
        for d in range(1, SUB):
            acc = acc + a[d * C:(d + 1) * C] * vds[d][:, p0:p0 + MXU_DIM]
        out.append(acc)
    return jnp.concatenate(out, axis=1)


def _hgrn_kernel(xb_ref, w_ref, lb_ref, ng_ref, o_ref, st_ref, *, rows, heads):
    DK = HGRN_DK

    @pl.when(pl.program_id(1) == 0)
    def _():
        st_ref[...] = jnp.zeros_like(st_ref)

    xb = xb_ref[...]
    prep = [_hgrn_prepare(xb, w_ref[h], lb_ref[h], rows) for h in range(heads)]
    in_range = jnp.min(prep[0]["b2"]) >= -HGRN_MAX_CHUNK_DECAY
    for p in prep[1:]:
        in_range = jnp.logical_and(in_range, jnp.min(p["b2"]) >= -HGRN_MAX_CHUNK_DECAY)

    def run(chunks_fn):
        for h, p in enumerate(prep):
            o, st = chunks_fn(p, st_ref[h])
            st_ref[h] = st
            o = o * lax.rsqrt(jnp.mean(o * o, axis=-1, keepdims=True) + RMS_EPS)
            gz = p["gz"]
            o_ref[:, h * DK:(h + 1) * DK] = (o * ng_ref[h] * (gz * _sigmoid(gz))).astype(BF16)

    @pl.when(in_range)
    def _():
        run(_hgrn_chunks_factored)

    @pl.when(jnp.logical_not(in_range))
    def _():
        run(_hgrn_chunks_robust)


def _hgrn(xb, lb, w_in, norm_g):
    S, D = xb.shape
    H, DK, HP = HGRN_HEADS, HGRN_DK, HGRN_HEADS_PER_STEP
    T = min(ROW_TILE, S)
    w = w_in.astype(BF16).reshape(D, 4, H, DK).transpose(2, 0, 1, 3).reshape(H, D, 4 * DK)
    return pl.pallas_call(
        functools.partial(_hgrn_kernel, rows=T, heads=HP),
        grid=(H // HP, S // T),
        in_specs=[
            pl.BlockSpec((T, D), lambda h, t: (t, 0)),
            pl.BlockSpec((HP, D, 4 * DK), lambda h, t: (h, 0, 0)),
            pl.BlockSpec((HP, 1, DK), lambda h, t: (h, 0, 0)),
            pl.BlockSpec((HP, 1, DK), lambda h, t: (h, 0, 0)),
        ],
        out_specs=pl.BlockSpec((T, HP * DK), lambda h, t: (t, h)),
        out_shape=jax.ShapeDtypeStruct((S, H * DK), BF16),
        scratch_shapes=[pltpu.VMEM((HP, DK, DK), F32)],
        compiler_params=_params("arbitrary", "arbitrary"),
        name="hgrn2",
    )(xb, w, lb.reshape(H, 1, DK), norm_g.reshape(H, 1, DK))


def _qkv_kernel(xb_ref, posr_ref, posc_ref, wqt_ref, wk_ref, wvt_ref, fcol_ref, frow_ref,
                qt_ref, k_ref, vt_ref, *, rows):
    H, HD, L = ATTN_HEADS, HEAD_DIM, MOBA_BLOCK
    x = xb_ref[...]
    qt = _dot_nt(wqt_ref[...], x)
    vt = _dot_nt(wvt_ref[...], x)
    k = _dot(x, wk_ref[...])

    ang_t = fcol_ref[...] * posr_ref[...].astype(F32)
    cos_t, sin_t = jnp.cos(ang_t), jnp.sin(ang_t)
    ang = posc_ref[...].astype(F32) * frow_ref[...]
    cos_r, sin_r = jnp.cos(ang), jnp.sin(ang)
    lane = lax.broadcasted_iota(jnp.int32, (rows, HD), 1)
    first = lane < ROPE_HALF
    sin_r = jnp.where(first, -sin_r, sin_r)

    for h in range(H):
        blk = qt[h * HD:(h + 1) * HD]
        x1, x2 = blk[:ROPE_HALF], blk[ROPE_HALF:ROPE_DIM]
        qh = jnp.concatenate([x1 * cos_t - x2 * sin_t, x2 * cos_t + x1 * sin_t, blk[ROPE_DIM:]], axis=0)
        qh = (qh * Q_SCALE).astype(BF16)
        kh = k[:, h * HD:(h + 1) * HD]
        partner = jnp.where(first, pltpu.roll(kh, HD - ROPE_HALF, 1), pltpu.roll(kh, ROPE_HALF, 1))
        kh = (kh * cos_r + partner * sin_r).astype(BF16)
        vh = vt[h * HD:(h + 1) * HD].astype(BF16)
        for n in range(rows // L):
            qt_ref[h, n] = qh[:, n * L:(n + 1) * L]
            vt_ref[h, n] = vh[:, n * L:(n + 1) * L]
            k_ref[h, n] = kh[n * L:(n + 1) * L]


def _attn_kernel(qt_ref, k_ref, vt_ref, o_ref, kmean_ref, sel_ref, s_ref, gmax_ref, m_ref, acc_ref, *, nblk):
    L = MOBA_BLOCK
    i = pl.program_id(1)

    G = ATTN_GROUP

    @pl.when(i == 0)
    def _():
        def mean_body(n, carry):
            kmean_ref[pl.ds(n, 1), :] = jnp.mean(k_ref[n].astype(F32), axis=0, keepdims=True)
            return carry
        lax.fori_loop(0, nblk, mean_body, 0)

    qt = qt_ref[...]
    km = kmean_ref[...]
    km_hi = km.astype(BF16)
    km_lo = (km - km_hi.astype(F32)).astype(BF16)
    gate = _dot(km_hi, qt) + _dot(km_lo, qt)

    kpos = lax.broadcasted_iota(jnp.int32, (L, L), 0)
    qpos = lax.broadcasted_iota(jnp.int32, (L, L), 1)
    s_own = jnp.where(kpos <= qpos, _dot(k_ref[i], qt), MASK_NEG)

    rowj = lax.broadcasted_iota(jnp.int32, (nblk, L), 0)
    gate = jnp.where(rowj < i, gate, -jnp.inf)
    sel = jnp.zeros((nblk, L), F32)
    for kk in range(MOBA_TOPK):
        mx = jnp.max(gate, axis=0, keepdims=True)
        idx = jnp.min(jnp.where(gate == mx, rowj, nblk), axis=0, keepdims=True)
        pick = jnp.logical_and(rowj == idx, kk < i)
        sel = jnp.where(pick, 1.0, sel)
        gate = jnp.where(pick, -jnp.inf, gate)
    sel_ref[:nblk] = sel

    def group_max(s):
        return jnp.max(s.reshape(L // SUBLANES, SUBLANES, L), axis=0)

    def with_ones(vt):
        return jnp.concatenate([vt, jnp.ones((BF16_SUBLANES, vt.shape[1]), BF16)], axis=0)

    m0 = jnp.max(s_own, axis=0, keepdims=True)
    m_ref[...] = m0
    acc_ref[...] = _dot(with_ones(vt_ref[i]), jnp.exp2(s_own - m0).astype(BF16))

    past = nblk - 1
    ngroups = pl.cdiv(past, G)
    half = G // 2

    def stage_scores(t, nb):
        buf, m8 = t % 2, None
        for u in range(min(nb, past - t * G)):
            jj = t * G + u
            s = jnp.where(sel_ref[jj:jj + 1, :] > 0.5, _dot(k_ref[jj], qt), MASK_NEG)
            s_ref[buf, u] = s
            m8 = group_max(s) if m8 is None else jnp.maximum(m8, group_max(s))
        gmax_ref[buf] = m8

    def stage_accumulate(t, nb):
        buf = t % 2
        m_old = m_ref[...]
        m_new = jnp.maximum(m_old, jnp.max(gmax_ref[buf], axis=0, keepdims=True))
        total = None
        for u in range(min(nb, past - t * G)):
            p = jnp.exp2(s_ref[buf, u] - m_new).astype(BF16)
            part = _dot(with_ones(vt_ref[t * G + u]), p)
            total = part if total is None else total + part
        acc_ref[...] = jnp.exp2(m_old - m_new) * acc_ref[...] + total
        m_ref[...] = m_new

    if ngroups > 0:
        stage_scores(0, G)
    for t in range(ngroups):
        real = i - t * G
        if t + 1 < ngroups:
            @pl.when(real - G > half)
            def _():
                stage_accumulate(t, G)
                stage_scores(t + 1, G)

            @pl.when(jnp.logical_and(real - G > 0, real - G <= half))
            def _():
                stage_accumulate(t, G)
                stage_scores(t + 1, half)

        @pl.when(jnp.logical_and(real > half, real <= G))
        def _():
            stage_accumulate(t, G)

        @pl.when(jnp.logical_and(real > 0, real <= half))
        def _():
            stage_accumulate(t, half)

    acc = acc_ref[...]
    o_ref[...] = (acc[:HEAD_DIM] / acc[HEAD_DIM:HEAD_DIM + 1]).T.astype(BF16)


def _moba(xb, positions, w_in):
    S, D = xb.shape
    H, HD, L = ATTN_HEADS, HEAD_DIM, MOBA_BLOCK
    assert S % L == 0
    NB = S // L
    T = min(ROW_TILE, S)
    nper = T // L
    wb = w_in.astype(BF16)
    wqt, wk, wvt = wb[:, :D].T, wb[:, D:2 * D], wb[:, 2 * D:].T
    inv_freq = 1.0 / (ROPE_THETA ** (jnp.arange(0, ROPE_DIM, 2, dtype=F32) / ROPE_DIM))
    fcol = inv_freq.reshape(ROPE_HALF, 1)
    frow = jnp.concatenate([inv_freq, inv_freq, jnp.zeros((HD - ROPE_DIM,), F32)]).reshape(1, HD)
    pos = positions.reshape(S).astype(jnp.int32)
    qt, k, vt = pl.pallas_call(
        functools.partial(_qkv_kernel, rows=T),
        grid=(S // T,),
        in_specs=[
            pl.BlockSpec((T, D), lambda i: (i, 0)),
            pl.BlockSpec((1, T), lambda i: (0, i)),
            pl.BlockSpec((T, 1), lambda i: (i, 0)),
            _const_spec((D, D)),
            _const_spec((D, D)),
            _const_spec((D, D)),
            _const_spec((ROPE_HALF, 1)),
            _const_spec((1, HD)),
        ],
        out_specs=[
            pl.BlockSpec((H, nper, HD, L), lambda i: (0, i, 0, 0)),
            pl.BlockSpec((H, nper, L, HD), lambda i: (0, i, 0, 0)),
            pl.BlockSpec((H, nper, HD, L), lambda i: (0, i, 0, 0)),
        ],
        out_shape=[
            jax.ShapeDtypeStruct((H, NB, HD, L), BF16),
            jax.ShapeDtypeStruct((H, NB, L, HD), BF16),
            jax.ShapeDtypeStruct((H, NB, HD, L), BF16),
        ],
        compiler_params=_params("arbitrary"),
        name="moba_qkv_rope",
    )(xb, pos.reshape(1, S), pos.reshape(S, 1), wqt, wk, wvt, fcol, frow)

    return pl.pallas_call(
        functools.partial(_attn_kernel, nblk=NB),
        grid=(H, NB),
        in_specs=[
            pl.BlockSpec((None, None, HD, L), lambda h, i: (h, i, 0, 0)),
            pl.BlockSpec((None, NB, L, HD), lambda h, i: (h, 0, 0, 0)),
            pl.BlockSpec((None, NB, HD, L), lambda h, i: (h, 0, 0, 0)),
        ],
        out_specs=pl.BlockSpec((L, HD), lambda h, i: (i, h)),
        out_shape=jax.ShapeDtypeStruct((S, H * HD), BF16),
        scratch_shapes=[
            pltpu.VMEM((NB, HD), F32),
            pltpu.VMEM((NB, L), F32),
            pltpu.VMEM((2, ATTN_GROUP, L, L), F32),
            pltpu.VMEM((2, SUBLANES, L), F32),
            pltpu.VMEM((1, L), F32),
            pltpu.VMEM((HD + BF16_SUBLANES, L), F32),
        ],
        compiler_params=_params("arbitrary", "arbitrary"),
        name="moba_attention",
    )(qt, k, vt)


def _hgrn_lower_bound(lower_bounds, layer):
    s = jax.nn.softmax(lower_bounds.astype(F32), axis=0)
    return (jnp.cumsum(s, axis=0) - s[0])[layer]


def kernel(x, positions, hgrn_lower_bounds, l0_mix_w_in, l0_mix_norm_g, l0_mix_w_out, l0_ln1_g, l0_ln1_b, l0_ffn_w_up, l0_ffn_conv, l0_ffn_w_down, l0_ln2_g, l0_ln2_b, l1_mix_w_in, l1_mix_w_out, l1_ln1_g, l1_ln1_b, l1_ffn_w_up, l1_ffn_conv, l1_ffn_w_down, l1_ln2_g, l1_ln2_b, l2_mix_w_in, l2_mix_conv, l2_mix_w_out, l2_ln1_g, l2_ln1_b, l2_ffn_w_up, l2_ffn_conv, l2_ffn_w_down, l2_ln2_g, l2_ln2_b, l3_mix_w_in, l3_mix_norm_g, l3_mix_w_out, l3_ln1_g, l3_ln1_b, l3_ffn_w_up, l3_ffn_conv, l3_ffn_w_down, l3_ln2_g, l3_ln2_b):
    layers = [
        ((l0_mix_w_in, l0_mix_norm_g, l0_mix_w_out), l0_ln1_g, l0_ln1_b,
         (l0_ffn_w_up, l0_ffn_conv, l0_ffn_w_down), l0_ln2_g, l0_ln2_b),
        ((l1_mix_w_in, l1_mix_w_out), l1_ln1_g, l1_ln1_b,
         (l1_ffn_w_up, l1_ffn_conv, l1_ffn_w_down), l1_ln2_g, l1_ln2_b),
        ((l2_mix_w_in, l2_mix_conv, l2_mix_w_out), l2_ln1_g, l2_ln1_b,
         (l2_ffn_w_up, l2_ffn_conv, l2_ffn_w_down), l2_ln2_g, l2_ln2_b),
        ((l3_mix_w_in, l3_mix_norm_g, l3_mix_w_out), l3_ln1_g, l3_ln1_b,
         (l3_ffn_w_up, l3_ffn_conv, l3_ffn_w_down), l3_ln2_g, l3_ln2_b),
    ]
    B, S, D = x.shape
    outs = []
    for bi in range(B):
        xf = x[bi]
        xb = xf.astype(BF16)
        for i in range(DEPTH):
            mix_p, ln1_g, ln1_b, ffn_p, ln2_g, ln2_b = layers[i]
            kind = i % N_MIXERS
            if kind == 0:
                w_in, norm_g, w_out = mix_p
                o = _hgrn(xb, _hgrn_lower_bound(hgrn_lower_bounds, i), w_in, norm_g)
                xf, xb = _ffn(xf, *ffn_p, ln2_g, ln2_b, mixer=(o, w_out, ln1_g, ln1_b))
            elif kind == 1:
                w_in, w_out = mix_p
                o = _moba(xb, positions[bi], w_in)
                xf, xb = _ffn(xf, *ffn_p, ln2_g, ln2_b, mixer=(o, w_out, ln1_g, ln1_b))
            else:
                w_in, conv_w, w_out = mix_p
                xf, xb = _conv_mixer(xb, xf, w_in, conv_w, w_out, ln1_g, ln1_b)
                xf, xb = _ffn(xf, *ffn_p, ln2_g, ln2_b, xb=xb)
        outs.append(xf)
    return outs[0][None] if B == 1 else jnp.stack(outs, axis=0)
```

```python
import functools
import math

import jax
import jax.numpy as jnp
from jax import lax
from jax.experimental import pallas as pl
from jax.experimental.pallas import tpu as pltpu

F32 = jnp.float32
BF16 = jnp.bfloat16

DEPTH = 4
N_MIXERS = 3
HGRN_HEADS = 8
HGRN_DK = 128
ATTN_HEADS = 8
HEAD_DIM = 128
MOBA_BLOCK = 256
MOBA_TOPK = 3
ROPE_THETA = 500000.0
ROPE_DIM = HEAD_DIM // 4
ROPE_HALF = ROPE_DIM // 2
LN_EPS = 1e-5
RMS_EPS = 1e-6
DEEPNORM_ALPHA = (2.0 * DEPTH) ** 0.25
MASK_NEG = -1e30

V7X_VMEM_BYTES = 64 * 1024 * 1024
VMEM_LIMIT_BYTES = V7X_VMEM_BYTES - 8 * 1024 * 1024
SUBLANES = 8
BF16_SUBLANES = 16
MXU_DIM = 256

ROW_TILE = 512
FFN_ROW_TILE = 1024
HGRN_CHUNK = 64
HGRN_SUB = SUBLANES
HGRN_MAX_CHUNK_DECAY = 80.0
HGRN_HEADS_PER_STEP = 8
ATTN_GROUP = 16
Q_SCALE = (1.0 / math.sqrt(HEAD_DIM)) * math.log2(math.e)


def _params(*sem):
    return pltpu.CompilerParams(dimension_semantics=sem, vmem_limit_bytes=VMEM_LIMIT_BYTES)


def _const_spec(shape):
    nd = len(shape)
    return pl.BlockSpec(shape, lambda *_: (0,) * nd, pipeline_mode=pl.Buffered(1))


def _sigmoid(x):
    return 1.0 / (1.0 + jnp.exp(-x))


def _layer_norm(y, g, b):
    mu = jnp.mean(y, axis=-1, keepdims=True)
    yc = y - mu
    var = jnp.mean(yc * yc, axis=-1, keepdims=True)
    return yc * lax.rsqrt(var + LN_EPS) * g + b


def _dot(a, b):
    return jnp.dot(a, b, preferred_element_type=F32)


def _dot_nt(a, b):
    return lax.dot_general(a, b, (((1,), (1,)), ((), ())), preferred_element_type=F32)


def _causal_conv3(u, w, carry_ref, col0):
    width = u.shape[1]
    cols = slice(col0, col0 + width)
    row = lax.broadcasted_iota(jnp.int32, (SUBLANES, width), 0)
    y = u * w[2:3]
    for j in (1, 2):
        r = pltpu.roll(u, j, 0)
        prev = carry_ref[j - 1, :, cols]
        carry_ref[j - 1, :, cols] = r[:SUBLANES]
        head = jnp.where(row < j, prev, r[:SUBLANES])
        y = y + jnp.concatenate([head, r[SUBLANES:]], axis=0) * w[2 - j:3 - j]
    return y


def _ffn_kernel(*refs, d_ff, tile, project):
    if project:
        (a_ref, wo_ref, g1_ref, b1_ref, xin_ref, wup_ref, cw_ref, wdn_ref, g_ref, b_ref,
         of_ref, ob_ref, carry_ref, h_ref, xf_ref) = refs
    else:
        (xb_ref, xf_ref, wup_ref, cw_ref, wdn_ref, g_ref, b_ref,
         of_ref, ob_ref, carry_ref, h_ref) = refs

    @pl.when(pl.program_id(0) == 0)
    def _():
        carry_ref[...] = jnp.zeros_like(carry_ref)

    if project:
        x = _layer_norm(DEEPNORM_ALPHA * xin_ref[...] + _dot(a_ref[...], wo_ref[...]),
                        g1_ref[...], b1_ref[...])
        xf_ref[...] = x
        xb = x.astype(BF16)
    else:
        xb = xb_ref[...]
    for c in range(d_ff // tile):
        ys = []
        for part in range(2):
            col0 = part * d_ff + c * tile
            u = _dot(xb, wup_ref[:, col0:col0 + tile])
            ys.append(_causal_conv3(u, cw_ref[:, col0:col0 + tile], carry_ref, col0))
        ya, yb = ys
        h_ref[:, c * tile:(c + 1) * tile] = (ya * _sigmoid(ya) * yb).astype(BF16)
    m = _dot(h_ref[...], wdn_ref[...])
    y = _layer_norm(DEEPNORM_ALPHA * xf_ref[...] + m, g_ref[...], b_ref[...])
    of_ref[...] = y
    ob_ref[...] = y.astype(BF16)


def _ffn(xf, w_up, conv_w, w_down, g, b, xb=None, mixer=None):
    S, D = xf.shape
    d_ff = w_down.shape[0]
    T = min(FFN_ROW_TILE, S)
    tile = MXU_DIM
    assert d_ff % tile == 0
    rows = lambda width: pl.BlockSpec((T, width), lambda i: (i, 0))
    vec = lambda v: v.reshape(1, D)
    ffn_specs = [_const_spec((D, 2 * d_ff)), _const_spec((3, 2 * d_ff)), _const_spec((d_ff, D)),
                 _const_spec((1, D)), _const_spec((1, D))]
    ffn_args = (w_up.astype(BF16), conv_w, w_down.astype(BF16), vec(g), vec(b))
    scratch = [pltpu.VMEM((2, SUBLANES, 2 * d_ff), F32), pltpu.VMEM((T, d_ff), BF16)]
    if mixer is None:
        in_specs = [rows(D), rows(D)] + ffn_specs
        args = (xb, xf) + ffn_args
    else:
        a, w_out, g1, b1 = mixer
        K = a.shape[1]
        in_specs = [rows(K), _const_spec((K, D)), _const_spec((1, D)), _const_spec((1, D)), rows(D)] + ffn_specs
        args = (a, w_out.astype(BF16), vec(g1), vec(b1), xf) + ffn_args
        scratch.append(pltpu.VMEM((T, D), F32))
    return pl.pallas_call(
        functools.partial(_ffn_kernel, d_ff=d_ff, tile=tile, project=mixer is not None),
        grid=(S // T,),
        in_specs=in_specs,
        out_specs=[rows(D), rows(D)],
        out_shape=[jax.ShapeDtypeStruct((S, D), F32), jax.ShapeDtypeStruct((S, D), BF16)],
        scratch_shapes=scratch,
        compiler_params=_params("arbitrary"),
        name="conv_ffn",
    )(*args)


def _conv_mixer_kernel(xb_ref, xf_ref, win_ref, cw_ref, wout_ref, g_ref, b_ref, of_ref, ob_ref,
                       carry_ref, y_ref, *, d_model, tile):
    @pl.when(pl.program_id(0) == 0)
    def _():
        carry_ref[...] = jnp.zeros_like(carry_ref)

    xb = xb_ref[...]
    for c in range(d_model // tile):
        col0 = c * tile
        bg = _dot(xb, win_ref[:, col0:col0 + tile])
        cg = _dot(xb, win_ref[:, d_model + col0:d_model + col0 + tile])
        hh = _dot(xb, win_ref[:, 2 * d_model + col0:2 * d_model + col0 + tile])
        conv = _causal_conv3(cg * hh, cw_ref[:, col0:col0 + tile], carry_ref, col0)
        y_ref[:, col0:col0 + tile] = (bg * conv).astype(BF16)
    m = _dot(y_ref[...], wout_ref[...])
    y = _layer_norm(DEEPNORM_ALPHA * xf_ref[...] + m, g_ref[...], b_ref[...])
    of_ref[...] = y
    ob_ref[...] = y.astype(BF16)


def _conv_mixer(xb, xf, w_in, conv_w, w_out, g, b):
    S, D = xf.shape
    T = min(ROW_TILE, S)
    return pl.pallas_call(
        functools.partial(_conv_mixer_kernel, d_model=D, tile=MXU_DIM),
        grid=(S // T,),
        in_specs=[
            pl.BlockSpec((T, D), lambda i: (i, 0)),
            pl.BlockSpec((T, D), lambda i: (i, 0)),
            _const_spec((D, 3 * D)),
            _const_spec((3, D)),
            _const_spec((D, D)),
            _const_spec((1, D)),
            _const_spec((1, D)),
        ],
        out_specs=[pl.BlockSpec((T, D), lambda i: (i, 0)), pl.BlockSpec((T, D), lambda i: (i, 0))],
        out_shape=[jax.ShapeDtypeStruct((S, D), F32), jax.ShapeDtypeStruct((S, D), BF16)],
        scratch_shapes=[
            pltpu.VMEM((2, SUBLANES, D), F32),
            pltpu.VMEM((T, D), BF16),
        ],
        compiler_params=_params("arbitrary"),
        name="conv_mixer",
    )(xb, xf, w_in.astype(BF16), conv_w, w_out.astype(BF16), g.reshape(1, D), b.reshape(1, D))


def _hgrn_prepare(xb, w, lb, rows):
    C, DK = HGRN_CHUNK, HGRN_DK
    nch = rows // C

    proj = _dot(xb, w)
    qz, fz, v, gz = (proj[:, j * DK:(j + 1) * DK] for j in range(4))
    q = qz * _sigmoid(qz)
    sig = _sigmoid(fz)
    k = (1.0 - lb) * (1.0 - sig)
    g = jnp.log(lb + (1.0 - lb) * sig)

    def to_lanes(a):
        return jnp.concatenate([a[c * C:(c + 1) * C] for c in range(nch)], axis=1)

    q2, k2, g2, v2 = to_lanes(q), to_lanes(k), to_lanes(g), to_lanes(v)

    ri = lax.broadcasted_iota(jnp.int32, (C, C), 0)
    ci = lax.broadcasted_iota(jnp.int32, (C, C), 1)
    tri = (ri >= ci).astype(BF16)
    g_hi = g2.astype(BF16)
    g_rem = g2 - g_hi.astype(F32)
    g_mid = g_rem.astype(BF16)
    g_lo = (g_rem - g_mid.astype(F32)).astype(BF16)
    b2 = _dot(tri, g_hi) + _dot(tri, g_mid) + _dot(tri, g_lo)
    return dict(q2=q2, k2=k2, v2=v2, b2=b2, gz=gz)


def _hgrn_chunk_decay_columns(b2):
    C, DK = HGRN_CHUNK, HGRN_DK
    nch = b2.shape[1] // DK
    b_last = b2[C - 1:C]
    rows = [b_last[:, c * DK:(c + 1) * DK] for c in range(nch)]
    padded = jnp.concatenate(rows + [jnp.zeros((DK - nch, DK), F32)], axis=0)
    return jnp.exp(padded.T)


def _hgrn_chunks_factored(p, st):
    C, DK = HGRN_CHUNK, HGRN_DK
    q2, k2, v2, b2 = p["q2"], p["k2"], p["v2"], p["b2"]
    qe = (q2 * jnp.exp(b2)).astype(BF16)
    kinv = k2 * jnp.exp(-b2)
    kinv_b = kinv.astype(BF16)
    v_b = v2.astype(BF16)
    causal = lax.broadcasted_iota(jnp.int32, (C, C), 0) >= lax.broadcasted_iota(jnp.int32, (C, C), 1)
    decay = _hgrn_chunk_decay_columns(b2)
    outs = []
    for c in range(q2.shape[1] // DK):
        sl = slice(c * DK, (c + 1) * DK)
        o_inter = _dot(qe[:, sl], st.astype(BF16))
        a = jnp.where(causal, _dot_nt(qe[:, sl], kinv_b[:, sl]), 0.0).astype(BF16)
        outs.append(o_inter + _dot(a, v_b[:, sl]))
        st = decay[:, c:c + 1] * (st + _dot(kinv[:, sl].T.astype(BF16), v_b[:, sl]))
    return jnp.concatenate(outs, axis=0), st


def _hgrn_chunks_robust(p, st):
    C, SUB, DK = HGRN_CHUNK, HGRN_SUB, HGRN_DK
    nsub = C // SUB
    q2, k2, v2, b2 = p["q2"], p["k2"], p["v2"], p["b2"]
    width = q2.shape[1]
    b_last = b2[C - 1:C]
    qe = (q2 * jnp.exp(b2)).astype(BF16)
    kdec = k2 * jnp.exp(b_last - b2)
    v_b = v2.astype(BF16)
    refs = [jnp.broadcast_to(b2[SUB * i - 1:SUB * i], (SUB, width)) for i in range(1, nsub)]
    rb = jnp.concatenate([jnp.zeros((SUB, width), F32)] + refs, axis=0)
    qp = (q2 * jnp.exp(b2 - rb)).astype(BF16)
    kst, vst = [], []
    for i in range(1, nsub):
        ref_i = jnp.concatenate([refs[i - 1]] * i, axis=0)
        kst.append(k2[:SUB * i] * jnp.exp(ref_i - b2[:SUB * i]))
        vst.append(v2[:SUB * i])
    kst = jnp.concatenate(kst, axis=0).astype(BF16)
    vst = jnp.concatenate(vst, axis=0).astype(BF16)
    nstack = kst.shape[0]
    row_blk = lax.broadcasted_iota(jnp.int32, (C, nstack), 0) // SUB
    col = lax.broadcasted_iota(jnp.int32, (C, nstack), 1)
    col_blk = jnp.ones((C, nstack), jnp.int32)
    for i in range(2, nsub):
        col_blk = col_blk + (col >= (SUB * i * (i - 1)) // 2).astype(jnp.int32)
    stack_mask = row_blk == col_blk
    o_diag = _hgrn_exact_diagonal(p)
    decay = _hgrn_chunk_decay_columns(b2)
    outs = []
    for c in range(width // DK):
        sl = slice(c * DK, (c + 1) * DK)
        r = jnp.where(stack_mask, _dot_nt(qp[:, sl], kst[:, sl]), 0.0).astype(BF16)
        outs.append(_dot(qe[:, sl], st.astype(BF16)) + _dot(r, vst[:, sl]) + o_diag[:, sl])
        st = decay[:, c:c + 1] * st + _dot(kdec[:, sl].T.astype(BF16), v_b[:, sl])
    return jnp.concatenate(outs, axis=0), st


def _hgrn_exact_diagonal(p):
    C, SUB, DK = HGRN_CHUNK, HGRN_SUB, HGRN_DK
    q2, k2, v2, b2 = p["q2"], p["k2"], p["v2"], p["b2"]
    width = q2.shape[1]
    trow = lax.broadcasted_iota(jnp.int32, (C, width), 0) % SUB
    ps, vds = [q2 * k2], [v2]
    for d in range(1, SUB):
        valid = trow >= d
        kd = pltpu.roll(k2, d, 0)
        bd = pltpu.roll(b2, d, 0)
        vds.append(pltpu.roll(v2, d, 0))
        e = jnp.where(valid, b2 - bd, 0.0)
        ps.append(jnp.where(valid, q2 * kd * jnp.exp(e), 0.0))
    li = lax.broadcasted_iota(jnp.int32, (MXU_DIM, MXU_DIM), 0) // DK
    lj = lax.broadcasted_iota(jnp.int32, (MXU_DIM, MXU_DIM), 1) // DK
    group_ones = (li == lj).astype(BF16)
    out = []
    for p0 in range(0, width, MXU_DIM):
        lhs = jnp.concatenate([x[:, p0:p0 + MXU_DIM] for x in ps], axis=0).astype(BF16)
        a = _dot(lhs, group_ones)
        acc = a[:C] * vds[0][:, p0:p0 + MXU_DIM]
        for d in range(1, SUB):
            acc = acc + a[d * C:(d + 1) * C] * vds[d][:, p0:p0 + MXU_DIM]
        out.append(acc)
    return jnp.concatenate(out, axis=1)


def _hgrn_kernel(xb_ref, w_ref, lb_ref, ng_ref, o_ref, st_ref, *, rows, heads):
    DK = HGRN_DK

    @pl.when(pl.program_id(1) == 0)
    def _():
        st_ref[...] = jnp.zeros_like(st_ref)

    xb = xb_ref[...]
    prep = [_hgrn_prepare(xb, w_ref[h], lb_ref[h], rows) for h in range(heads)]
    in_range = jnp.min(prep[0]["b2"]) >= -HGRN_MAX_CHUNK_DECAY
    for p in prep[1:]:
        in_range = jnp.logical_and(in_range, jnp.min(p["b2"]) >= -HGRN_MAX_CHUNK_DECAY)

    def run(chunks_fn):
        for h, p in enumerate(prep):
            o, st = chunks_fn(p, st_ref[h])
            st_ref[h] = st
            o = o * lax.rsqrt(jnp.mean(o * o, axis=-1, keepdims=True) + RMS_EPS)
            gz = p["gz"]
            o_ref[:, h * DK:(h + 1) * DK] = (o * ng_ref[h] * (gz * _sigmoid(gz))).astype(BF16)

    @pl.when(in_range)
    def _():
        run(_hgrn_chunks_factored)

    @pl.when(jnp.logical_not(in_range))
    def _():
        run(_hgrn_chunks_robust)


def _hgrn(xb, lb, w_in, norm_g):
    S, D = xb.shape
    H, DK, HP = HGRN_HEADS, HGRN_DK, HGRN_HEADS_PER_STEP
    T = min(ROW_TILE, S)
    w = w_in.astype(BF16).reshape(D, 4, H, DK).transpose(2, 0, 1, 3).reshape(H, D, 4 * DK)
    return pl.pallas_call(
        functools.partial(_hgrn_kernel, rows=T, heads=HP),
        grid=(H // HP, S // T),
        in_specs=[
            pl.BlockSpec((T, D), lambda h, t: (t, 0)),
            pl.BlockSpec((HP, D, 4 * DK), lambda h, t: (h, 0, 0)),
            pl.BlockSpec((HP, 1, DK), lambda h, t: (h, 0, 0)),
            pl.BlockSpec((HP, 1, DK), lambda h, t: (h, 0, 0)),
        ],
        out_specs=pl.BlockSpec((T, HP * DK), lambda h, t: (t, h)),
        out_shape=jax.ShapeDtypeStruct((S, H * DK), BF16),
        scratch_shapes=[pltpu.VMEM((HP, DK, DK), F32)],
        compiler_params=_params("arbitrary", "arbitrary"),
        name="hgrn2",
    )(xb, w, lb.reshape(H, 1, DK), norm_g.reshape(H, 1, DK))


def _qkv_kernel(xb_ref, posr_ref, posc_ref, wqt_ref, wk_ref, wvt_ref, fcol_ref, frow_ref,
                qt_ref, k_ref, vt_ref, *, rows):
    H, HD, L = ATTN_HEADS, HEAD_DIM, MOBA_BLOCK
    x = xb_ref[...]
    qt = _dot_nt(wqt_ref[...], x)
    vt = _dot_nt(wvt_ref[...], x)
    k = _dot(x, wk_ref[...])

    ang_t = fcol_ref[...] * posr_ref[...].astype(F32)
    cos_t, sin_t = jnp.cos(ang_t), jnp.sin(ang_t)
    ang = posc_ref[...].astype(F32) * frow_ref[...]
    cos_r, sin_r = jnp.cos(ang), jnp.sin(ang)
    lane = lax.broadcasted_iota(jnp.int32, (rows, HD), 1)
    first = lane < ROPE_HALF
    sin_r = jnp.where(first, -sin_r, sin_r)

    for h in range(H):
        blk = qt[h * HD:(h + 1) * HD]
        x1, x2 = blk[:ROPE_HALF], blk[ROPE_HALF:ROPE_DIM]
        qh = jnp.concatenate([x1 * cos_t - x2 * sin_t, x2 * cos_t + x1 * sin_t, blk[ROPE_DIM:]], axis=0)
        qh = (qh * Q_SCALE).astype(BF16)
        kh = k[:, h * HD:(h + 1) * HD]
        partner = jnp.where(first, pltpu.roll(kh, HD - ROPE_HALF, 1), pltpu.roll(kh, ROPE_HALF, 1))
        kh = (kh * cos_r + partner * sin_r).astype(BF16)
        vh = vt[h * HD:(h + 1) * HD].astype(BF16)
        for n in range(rows // L):
            qt_ref[h, n] = qh[:, n * L:(n + 1) * L]
            vt_ref[h, n] = vh[:, n * L:(n + 1) * L]
            k_ref[h, n] = kh[n * L:(n + 1) * L]


def _attn_kernel(qt_ref, k_ref, vt_ref, o_ref, kmean_ref, sel_ref, s_ref, gmax_ref, m_ref, acc_ref, *, nblk):
    L = MOBA_BLOCK
    i = pl.program_id(1)

    G = ATTN_GROUP

    @pl.when(i == 0)
    def _():
        def mean_body(n, carry):
            kmean_ref[pl.ds(n, 1), :] = jnp.mean(k_ref[n].astype(F32), axis=0, keepdims=True)
            return carry
        lax.fori_loop(0, nblk, mean_body, 0)

    qt = qt_ref[...]
    km = kmean_ref[...]
    km_hi = km.astype(BF16)
    km_lo = (km - km_hi.astype(F32)).astype(BF16)
    gate = _dot(km_hi, qt) + _dot(km_lo, qt)

    kpos = lax.broadcasted_iota(jnp.int32, (L, L), 0)
    qpos = lax.broadcasted_iota(jnp.int32, (L, L), 1)
    s_own = jnp.where(kpos <= qpos, _dot(k_ref[i], qt), MASK_NEG)

    rowj = lax.broadcasted_iota(jnp.int32, (nblk, L), 0)
    gate = jnp.where(rowj < i, gate, -jnp.inf)
    sel = jnp.zeros((nblk, L), F32)
    for kk in range(MOBA_TOPK):
        mx = jnp.max(gate, axis=0, keepdims=True)
        idx = jnp.min(jnp.where(gate == mx, rowj, nblk), axis=0, keepdims=True)
        pick = jnp.logical_and(rowj == idx, kk < i)
        sel = jnp.where(pick, 1.0, sel)
        gate = jnp.where(pick, -jnp.inf, gate)
    sel_ref[:nblk] = sel

    def group_max(s):
        return jnp.max(s.reshape(L // SUBLANES, SUBLANES, L), axis=0)

    def with_ones(vt):
        return jnp.concatenate([vt, jnp.ones((BF16_SUBLANES, vt.shape[1]), BF16)], axis=0)

    m0 = jnp.max(s_own, axis=0, keepdims=True)
    m_ref[...] = m0
    acc_ref[...] = _dot(with_ones(vt_ref[i]), jnp.exp2(s_own - m0).astype(BF16))

    past = nblk - 1
    ngroups = pl.cdiv(past, G)
    half = G // 2

    def stage_scores(t, nb):
        buf, m8 = t % 2, None
        for u in range(min(nb, past - t * G)):
            jj = t * G + u
            s = jnp.where(sel_ref[jj:jj + 1, :] > 0.5, _dot(k_ref[jj], qt), MASK_NEG)
            s_ref[buf, u] = s
            m8 = group_max(s) if m8 is None else jnp.maximum(m8, group_max(s))
        gmax_ref[buf] = m8

    def stage_accumulate(t, nb):
        buf = t % 2
        m_old = m_ref[...]
        m_new = jnp.maximum(m_old, jnp.max(gmax_ref[buf], axis=0, keepdims=True))
        total = None
        for u in range(min(nb, past - t * G)):
            p = jnp.exp2(s_ref[buf, u] - m_new).astype(BF16)
            part = _dot(with_ones(vt_ref[t * G + u]), p)
            total = part if total is None else total + part
        acc_ref[...] = jnp.exp2(m_old - m_new) * acc_ref[...] + total
        m_ref[...] = m_new

    if ngroups > 0:
        stage_scores(0, G)
    for t in range(ngroups):
        real = i - t * G
        if t + 1 < ngroups:
            @pl.when(real - G > half)
            def _():
                stage_accumulate(t, G)
                stage_scores(t + 1, G)

            @pl.when(jnp.logical_and(real - G > 0, real - G <= half))
            def _():
                stage_accumulate(t, G)
                stage_scores(t + 1, half)

        @pl.when(jnp.logical_and(real > half, real <= G))
        def _():
            stage_accumulate(t, G)

        @pl.when(jnp.logical_and(real > 0, real <= half))
        def _():
            stage_accumulate(t, half)

    acc = acc_ref[...]
    o_ref[...] = (acc[:HEAD_DIM] / acc[HEAD_DIM:HEAD_DIM + 1]).T.astype(BF16)


def _moba(xb, positions, w_in):
    S, D = xb.shape
    H, HD, L = ATTN_HEADS, HEAD_DIM, MOBA_BLOCK
    assert S % L == 0
    NB = S // L
    T = min(ROW_TILE, S)
    nper = T // L
    wb = w_in.astype(BF16)
    wqt, wk, wvt = wb[:, :D].T, wb[:, D:2 * D], wb[:, 2 * D:].T
    inv_freq = 1.0 / (ROPE_THETA ** (jnp.arange(0, ROPE_DIM, 2, dtype=F32) / ROPE_DIM))
    fcol = inv_freq.reshape(ROPE_HALF, 1)
    frow = jnp.concatenate([inv_freq, inv_freq, jnp.zeros((HD - ROPE_DIM,), F32)]).reshape(1, HD)
    pos = positions.reshape(S).astype(jnp.int32)
    qt, k, vt = pl.pallas_call(
        functools.partial(_qkv_kernel, rows=T),
        grid=(S // T,),
        in_specs=[
            pl.BlockSpec((T, D), lambda i: (i, 0)),
            pl.BlockSpec((1, T), lambda i: (0, i)),
            pl.BlockSpec((T, 1), lambda i: (i, 0)),
            _const_spec((D, D)),
            _const_spec((D, D)),
            _const_spec((D, D)),
            _const_spec((ROPE_HALF, 1)),
            _const_spec((1, HD)),
        ],
        out_specs=[
            pl.BlockSpec((H, nper, HD, L), lambda i: (0, i, 0, 0)),
            pl.BlockSpec((H, nper, L, HD), lambda i: (0, i, 0, 0)),
            pl.BlockSpec((H, nper, HD, L), lambda i: (0, i, 0, 0)),
        ],
        out_shape=[
            jax.ShapeDtypeStruct((H, NB, HD, L), BF16),
            jax.ShapeDtypeStruct((H, NB, L, HD), BF16),
            jax.ShapeDtypeStruct((H, NB, HD, L), BF16),
        ],
        compiler_params=_params("arbitrary"),
        name="moba_qkv_rope",
    )(xb, pos.reshape(1, S), pos.reshape(S, 1), wqt, wk, wvt, fcol, frow)

    return pl.pallas_call(
        functools.partial(_attn_kernel, nblk=NB),
        grid=(H, NB),
        in_specs=[
            pl.BlockSpec((None, None, HD, L), lambda h, i: (h, i, 0, 0)),
            pl.BlockSpec((None, NB, L, HD), lambda h, i: (h, 0, 0, 0)),
            pl.BlockSpec((None, NB, HD, L), lambda h, i: (h, 0, 0, 0)),
        ],
        out_specs=pl.BlockSpec((L, HD), lambda h, i: (i, h)),
        out_shape=jax.ShapeDtypeStruct((S, H * HD), BF16),
        scratch_shapes=[
            pltpu.VMEM((NB, HD), F32),
            pltpu.VMEM((NB, L), F32),
            pltpu.VMEM((2, ATTN_GROUP, L, L), F32),
            pltpu.VMEM((2, SUBLANES, L), F32),
            pltpu.VMEM((1, L), F32),
            pltpu.VMEM((HD + BF16_SUBLANES, L), F32),
        ],
        compiler_params=_params("arbitrary", "arbitrary"),
        name="moba_attention",
    )(qt, k, vt)


def _hgrn_lower_bound(lower_bounds, layer):
    s = jax.nn.softmax(lower_bounds.astype(F32), axis=0)
    return (jnp.cumsum(s, axis=0) - s[0])[layer]


def kernel(x, positions, hgrn_lower_bounds, l0_mix_w_in, l0_mix_norm_g, l0_mix_w_out, l0_ln1_g, l0_ln1_b, l0_ffn_w_up, l0_ffn_conv, l0_ffn_w_down, l0_ln2_g, l0_ln2_b, l1_mix_w_in, l1_mix_w_out, l1_ln1_g, l1_ln1_b, l1_ffn_w_up, l1_ffn_conv, l1_ffn_w_down, l1_ln2_g, l1_ln2_b, l2_mix_w_in, l2_mix_conv, l2_mix_w_out, l2_ln1_g, l2_ln1_b, l2_ffn_w_up, l2_ffn_conv, l2_ffn_w_down, l2_ln2_g, l2_ln2_b, l3_mix_w_in, l3_mix_norm_g, l3_mix_w_out, l3_ln1_g, l3_ln1_b, l3_ffn_w_up, l3_ffn_conv, l3_ffn_w_down, l3_ln2_g, l3_ln2_b):
    layers = [
        ((l0_mix_w_in, l0_mix_norm_g, l0_mix_w_out), l0_ln1_g, l0_ln1_b,
         (l0_ffn_w_up, l0_ffn_conv, l0_ffn_w_down), l0_ln2_g, l0_ln2_b),
        ((l1_mix_w_in, l1_mix_w_out), l1_ln1_g, l1_ln1_b,
         (l1_ffn_w_up, l1_ffn_conv, l1_ffn_w_down), l1_ln2_g, l1_ln2_b),
        ((l2_mix_w_in, l2_mix_conv, l2_mix_w_out), l2_ln1_g, l2_ln1_b,
         (l2_ffn_w_up, l2_ffn_conv, l2_ffn_w_down), l2_ln2_g, l2_ln2_b),
        ((l3_mix_w_in, l3_mix_norm_g, l3_mix_w_out), l3_ln1_g, l3_ln1_b,
         (l3_ffn_w_up, l3_ffn_conv, l3_ffn_w_down), l3_ln2_g, l3_ln2_b),
    ]
    B, S, D = x.shape
    outs = []
    for bi in range(B):
        xf = x[bi]
        xb = xf.astype(BF16)
        for i in range(DEPTH):
            mix_p, ln1_g, ln1_b, ffn_p, ln2_g, ln2_b = layers[i]
            kind = i % N_MIXERS
            if kind == 0:
                w_in, norm_g, w_out = mix_p
                o = _hgrn(xb, _hgrn_lower_bound(hgrn_lower_bounds, i), w_in, norm_g)
                xf, xb = _ffn(xf, *ffn_p, ln2_g, ln2_b, mixer=(o, w_out, ln1_g, ln1_b))
            elif kind == 1:
                w_in, w_out = mix_p
                o = _moba(xb, positions[bi], w_in)
                xf, xb = _ffn(xf, *ffn_p, ln2_g, ln2_b, mixer=(o, w_out, ln1_g, ln1_b))
            else:
                w_in, conv_w, w_out = mix_p
                xf, xb = _conv_mixer(xb, xf, w_in, conv_w, w_out, ln1_g, ln1_b)
                xf, xb = _ffn(xf, *ffn_p, ln2_g, ln2_b, xb=xb)
        outs.append(xf)
    return outs[0][None] if B == 1 else jnp.stack(outs, axis=0)
```

```python
import functools
import math

import jax
import jax.numpy as jnp
from jax import lax
from jax.experimental import pallas as pl
from jax.experimental.pallas import tpu as pltpu

F32 = jnp.float32
BF16 = jnp.bfloat16

DEPTH = 4
N_MIXERS = 3
HGRN_HEADS = 8
HGRN_DK = 128
ATTN_HEADS = 8
HEAD_DIM = 128
MOBA_BLOCK = 256
MOBA_TOPK = 3
ROPE_THETA = 500000.0
ROPE_DIM = HEAD_DIM // 4
ROPE_HALF = ROPE_DIM // 2
LN_EPS = 1e-5
RMS_EPS = 1e-6
DEEPNORM_ALPHA = (2.0 * DEPTH) ** 0.25
MASK_NEG = -1e30

V7X_VMEM_BYTES = 64 * 1024 * 1024
VMEM_LIMIT_BYTES = V7X_VMEM_BYTES - 8 * 1024 * 1024
SUBLANES = 8
BF16_SUBLANES = 16
LANES = 128
MXU_DIM = 256

ROW_TILE = 512
FFN_ROW_TILE = 512
HGRN_CHUNK = 64
HGRN_SUB = SUBLANES
HGRN_MAX_CHUNK_DECAY = 80.0
HGRN_HEADS_PER_STEP = 4
ATTN_GROUP = 16
Q_SCALE = (1.0 / math.sqrt(HEAD_DIM)) * math.log2(math.e)


def _params(*sem):
    return pltpu.CompilerParams(dimension_semantics=sem, vmem_limit_bytes=VMEM_LIMIT_BYTES)


def _const_spec(shape):
    nd = len(shape)
    return pl.BlockSpec(shape, lambda *_: (0,) * nd, pipeline_mode=pl.Buffered(1))


def _sigmoid(x):
    return 1.0 / (1.0 + jnp.exp(-x))


def _layer_norm(y, g, b):
    mu = jnp.mean(y, axis=-1, keepdims=True)
    yc = y - mu
    var = jnp.mean(yc * yc, axis=-1, keepdims=True)
    return yc * lax.rsqrt(var + LN_EPS) * g + b


def _dot(a, b):
    return jnp.dot(a, b, preferred_element_type=F32)


def _dot_nt(a, b):
    return lax.dot_general(a, b, (((1,), (1,)), ((), ())), preferred_element_type=F32)


def _causal_conv3(u, w, carry_ref, col0):
    width = u.shape[1]
    cols = slice(col0, col0 + width)
    row = lax.broadcasted_iota(jnp.int32, (SUBLANES, width), 0)
    y = u * w[2:3]
    for j in (1, 2):
        r = pltpu.roll(u, j, 0)
        prev = carry_ref[j - 1, :, cols]
        carry_ref[j - 1, :, cols] = r[:SUBLANES]
        head = jnp.where(row < j, prev, r[:SUBLANES])
        y = y + jnp.concatenate([head, r[SUBLANES:]], axis=0) * w[2 - j:3 - j]
    return y


def _ffn_kernel(*refs, d_ff, tile, project, conv_mixer):
    if conv_mixer:
        (xin_b_ref, win_ref, mcw_ref, wo_ref, g1_ref, b1_ref, xin_ref, wup_ref, cw_ref, wdn_ref, g_ref, b_ref,
         of_ref, ob_ref, carry_ref, h_ref, xf_ref, mcarry_ref, a_ref) = refs
    elif project:
        (a_ref, wo_ref, g1_ref, b1_ref, xin_ref, wup_ref, cw_ref, wdn_ref, g_ref, b_ref,
         of_ref, ob_ref, carry_ref, h_ref, xf_ref) = refs
    else:
        (xb_ref, xf_ref, wup_ref, cw_ref, wdn_ref, g_ref, b_ref,
         of_ref, ob_ref, carry_ref, h_ref) = refs

    @pl.when(pl.program_id(0) == 0)
    def _():
        carry_ref[...] = jnp.zeros_like(carry_ref)
        if conv_mixer:
            mcarry_ref[...] = jnp.zeros_like(mcarry_ref)

    if conv_mixer:
        d_model = wo_ref.shape[0]
        xin_b = xin_b_ref[...]
        for c in range(d_model // tile):
            col0 = c * tile
            bg = _dot(xin_b, win_ref[:, col0:col0 + tile])
            cg = _dot(xin_b, win_ref[:, d_model + col0:d_model + col0 + tile])
            hh = _dot(xin_b, win_ref[:, 2 * d_model + col0:2 * d_model + col0 + tile])
            conv = _causal_conv3(cg * hh, mcw_ref[:, col0:col0 + tile], mcarry_ref, col0)
            a_ref[:, col0:col0 + tile] = (bg * conv).astype(BF16)
    if project:
        x = _layer_norm(DEEPNORM_ALPHA * xin_ref[...] + _dot(a_ref[...], wo_ref[...]),
                        g1_ref[...], b1_ref[...])
        xf_ref[...] = x
        xb = x.astype(BF16)
    else:
        xb = xb_ref[...]
    for c in range(d_ff // tile):
        ys = []
        for part in range(2):
            col0 = part * d_ff + c * tile
            u = _dot(xb, wup_ref[:, col0:col0 + tile])
            ys.append(_causal_conv3(u, cw_ref[:, col0:col0 + tile], carry_ref, col0))
        ya, yb = ys
        h_ref[:, c * tile:(c + 1) * tile] = (ya * _sigmoid(ya) * yb).astype(BF16)
    m = _dot(h_ref[...], wdn_ref[...])
    y = _layer_norm(DEEPNORM_ALPHA * xf_ref[...] + m, g_ref[...], b_ref[...])
    of_ref[...] = y
    ob_ref[...] = y.astype(BF16)


def _ffn(xf, w_up, conv_w, w_down, g, b, xb=None, mixer=None):
    S, D = xf.shape
    d_ff = w_down.shape[0]
    T = min(FFN_ROW_TILE, S)
    tile = MXU_DIM
    assert d_ff % tile == 0
    rows = lambda width: pl.BlockSpec((T, width), lambda i: (i, 0))
    vec = lambda v: v.reshape(1, D)
    ffn_specs = [_const_spec((D, 2 * d_ff)), _const_spec((3, 2 * d_ff)), _const_spec((d_ff, D)),
                 _const_spec((1, D)), _const_spec((1, D))]
    ffn_args = (w_up.astype(BF16), conv_w, w_down.astype(BF16), vec(g), vec(b))
    scratch = [pltpu.VMEM((2, SUBLANES, 2 * d_ff), F32), pltpu.VMEM((T, d_ff), BF16)]
    conv_mixer = mixer is not None and len(mixer) == 6
    if mixer is None:
        in_specs = [rows(D), rows(D)] + ffn_specs
        args = (xb, xf) + ffn_args
    elif conv_mixer:
        xin_b, w_in, mix_conv, w_out, g1, b1 = mixer
        in_specs = [rows(D), _const_spec((D, 3 * D)), _const_spec((3, D)), _const_spec((D, D)),
                    _const_spec((1, D)), _const_spec((1, D)), rows(D)] + ffn_specs
        args = (xin_b, w_in.astype(BF16), mix_conv, w_out.astype(BF16), vec(g1), vec(b1), xf) + ffn_args
        scratch += [pltpu.VMEM((T, D), F32), pltpu.VMEM((2, SUBLANES, D), F32), pltpu.VMEM((T, D), BF16)]
    else:
        a, w_out, g1, b1 = mixer
        K = a.shape[1]
        in_specs = [rows(K), _const_spec((K, D)), _const_spec((1, D)), _const_spec((1, D)), rows(D)] + ffn_specs
        args = (a, w_out.astype(BF16), vec(g1), vec(b1), xf) + ffn_args
        scratch.append(pltpu.VMEM((T, D), F32))
    return pl.pallas_call(
        functools.partial(_ffn_kernel, d_ff=d_ff, tile=tile, project=mixer is not None, conv_mixer=conv_mixer),
        grid=(S // T,),
        in_specs=in_specs,
        out_specs=[rows(D), rows(D)],
        out_shape=[jax.ShapeDtypeStruct((S, D), F32), jax.ShapeDtypeStruct((S, D), BF16)],
        scratch_shapes=scratch,
        compiler_params=_params("arbitrary"),
        name="conv_ffn",
    )(*args)


def _hgrn_prepare(xb, w, lb, rows):
    C, DK = HGRN_CHUNK, HGRN_DK
    nch = rows // C

    proj = _dot(xb, w)
    qz, fz, v, gz = (proj[:, j * DK:(j + 1) * DK] for j in range(4))
    q = qz * _sigmoid(qz)
    sig = _sigmoid(fz)
    k = (1.0 - lb) * (1.0 - sig)
    g = jnp.log(lb + (1.0 - lb) * sig)

    def to_lanes(a):
        return jnp.concatenate([a[c * C:(c + 1) * C] for c in range(nch)], axis=1)

    q2, k2, g2, v2 = to_lanes(q), to_lanes(k), to_lanes(g), to_lanes(v)

    ri = lax.broadcasted_iota(jnp.int32, (C, C), 0)
    ci = lax.broadcasted_iota(jnp.int32, (C, C), 1)
    tri = (ri >= ci).astype(BF16)
    g_hi = g2.astype(BF16)
    g_rem = g2 - g_hi.astype(F32)
    g_mid = g_rem.astype(BF16)
    g_lo = (g_rem - g_mid.astype(F32)).astype(BF16)
    b2 = _dot(tri, g_hi) + _dot(tri, g_mid) + _dot(tri, g_lo)
    return dict(q2=q2, k2=k2, v2=v2, b2=b2, gz=gz)


def _hgrn_chunk_decay_columns(b2):
    C, DK = HGRN_CHUNK, HGRN_DK
    nch = b2.shape[1] // DK
    b_last = b2[C - 1:C]
    rows = [b_last[:, c * DK:(c + 1) * DK] for c in range(nch)]
    padded = jnp.concatenate(rows + [jnp.zeros((DK - nch, DK), F32)], axis=0)
    return jnp.exp(padded.T)


def _hgrn_chunks_factored(p, st):
    C, DK = HGRN_CHUNK, HGRN_DK
    q2, k2, v2, b2 = p["q2"], p["k2"], p["v2"], p["b2"]
    qe = (q2 * jnp.exp(b2)).astype(BF16)
    kinv = k2 * jnp.exp(-b2)
    kinv_b = kinv.astype(BF16)
    v_b = v2.astype(BF16)
    causal = lax.broadcasted_iota(jnp.int32, (C, C), 0) >= lax.broadcasted_iota(jnp.int32, (C, C), 1)
    decay = _hgrn_chunk_decay_columns(b2)
    outs = []
    for c in range(q2.shape[1] // DK):
        sl = slice(c * DK, (c + 1) * DK)
        o_inter = _dot(qe[:, sl], st.astype(BF16))
        a = jnp.where(causal, _dot_nt(qe[:, sl], kinv_b[:, sl]), 0.0).astype(BF16)
        outs.append(o_inter + _dot(a, v_b[:, sl]))
        st = decay[:, c:c + 1] * (st + _dot(kinv[:, sl].T.astype(BF16), v_b[:, sl]))
    return jnp.concatenate(outs, axis=0), st


def _hgrn_chunks_robust(p, st):
    C, SUB, DK = HGRN_CHUNK, HGRN_SUB, HGRN_DK
    nsub = C // SUB
    q2, k2, v2, b2 = p["q2"], p["k2"], p["v2"], p["b2"]
    width = q2.shape[1]
    b_last = b2[C - 1:C]
    qe = (q2 * jnp.exp(b2)).astype(BF16)
    kdec = k2 * jnp.exp(b_last - b2)
    v_b = v2.astype(BF16)
    refs = [jnp.broadcast_to(b2[SUB * i - 1:SUB * i], (SUB, width)) for i in range(1, nsub)]
    rb = jnp.concatenate([jnp.zeros((SUB, width), F32)] + refs, axis=0)
    qp = (q2 * jnp.exp(b2 - rb)).astype(BF16)
    kst, vst = [], []
    for i in range(1, nsub):
        ref_i = jnp.concatenate([refs[i - 1]] * i, axis=0)
        kst.append(k2[:SUB * i] * jnp.exp(ref_i - b2[:SUB * i]))
        vst.append(v2[:SUB * i])
    kst = jnp.concatenate(kst, axis=0).astype(BF16)
    vst = jnp.concatenate(vst, axis=0).astype(BF16)
    nstack = kst.shape[0]
    row_blk = lax.broadcasted_iota(jnp.int32, (C, nstack), 0) // SUB
    col = lax.broadcasted_iota(jnp.int32, (C, nstack), 1)
    col_blk = jnp.ones((C, nstack), jnp.int32)
    for i in range(2, nsub):
        col_blk = col_blk + (col >= (SUB * i * (i - 1)) // 2).astype(jnp.int32)
    stack_mask = row_blk == col_blk
    o_diag = _hgrn_exact_diagonal(p)
    decay = _hgrn_chunk_decay_columns(b2)
    outs = []
    for c in range(width // DK):
        sl = slice(c * DK, (c + 1) * DK)
        r = jnp.where(stack_mask, _dot_nt(qp[:, sl], kst[:, sl]), 0.0).astype(BF16)
        outs.append(_dot(qe[:, sl], st.astype(BF16)) + _dot(r, vst[:, sl]) + o_diag[:, sl])
        st = decay[:, c:c + 1] * st + _dot(kdec[:, sl].T.astype(BF16), v_b[:, sl])
    return jnp.concatenate(outs, axis=0), st


def _hgrn_exact_diagonal(p):
    C, SUB, DK = HGRN_CHUNK, HGRN_SUB, HGRN_DK
    q2, k2, v2, b2 = p["q2"], p["k2"], p["v2"], p["b2"]
    width = q2.shape[1]
    trow = lax.broadcasted_iota(jnp.int32, (C, width), 0) % SUB
    ps, vds = [q2 * k2], [v2]
    for d in range(1, SUB):
        valid = trow >= d
        kd = pltpu.roll(k2, d, 0)
        bd = pltpu.roll(b2, d, 0)
        vds.append(pltpu.roll(v2, d, 0))
        e = jnp.where(valid, b2 - bd, 0.0)
        ps.append(jnp.where(valid, q2 * kd * jnp.exp(e), 0.0))
    li = lax.broadcasted_iota(jnp.int32, (MXU_DIM, MXU_DIM), 0) // DK
    lj = lax.broadcasted_iota(jnp.int32, (MXU_DIM, MXU_DIM), 1) // DK
    group_ones = (li == lj).astype(BF16)
    out = []
    for p0 in range(0, width, MXU_DIM):
        lhs = jnp.concatenate([x[:, p0:p0 + MXU_DIM] for x in ps], axis=0).astype(BF16)
        a = _dot(lhs, group_ones)
        acc = a[:C] * vds[0][:, p0:p0 + MXU_DIM]
        for d in range(1, SUB):
            acc = acc + a[d * C:(d + 1) * C] * vds[d][:, p0:p0 + MXU_DIM]
        out.append(acc)
    return jnp.concatenate(out, axis=1)


def _hgrn_kernel(xb_ref, w_ref, lb_ref, ng_ref, o_ref, st_ref, *, rows, heads):
    DK = HGRN_DK

    @pl.when(pl.program_id(1) == 0)
    def _():
        st_ref[...] = jnp.zeros_like(st_ref)

    xb = xb_ref[...]
    prep = [_hgrn_prepare(xb, w_ref[h], lb_ref[h], rows) for h in range(heads)]
    in_range = jnp.min(prep[0]["b2"]) >= -HGRN_MAX_CHUNK_DECAY
    for p in prep[1:]:
        in_range = jnp.logical_and(in_range, jnp.min(p["b2"]) >= -HGRN_MAX_CHUNK_DECAY)

    def run(chunks_fn):
        for h, p in enumerate(prep):
            o, st = chunks_fn(p, st_ref[h])
            st_ref[h] = st
            o = o * lax.rsqrt(jnp.mean(o * o, axis=-1, keepdims=True) + RMS_EPS)
            gz = p["gz"]
            o_ref[:, h * DK:(h + 1) * DK] = (o * ng_ref[h] * (gz * _sigmoid(gz))).astype(BF16)

    @pl.when(in_range)
    def _():
        run(_hgrn_chunks_factored)

    @pl.when(jnp.logical_not(in_range))
    def _():
        run(_hgrn_chunks_robust)


def _hgrn(xb, lb, w_in, norm_g):
    S, D = xb.shape
    H, DK, HP = HGRN_HEADS, HGRN_DK, HGRN_HEADS_PER_STEP
    T = min(ROW_TILE, S)
    w = w_in.astype(BF16).reshape(D, 4, H, DK).transpose(2, 0, 1, 3).reshape(H, D, 4 * DK)
    return pl.pallas_call(
        functools.partial(_hgrn_kernel, rows=T, heads=HP),
        grid=(H // HP, S // T),
        in_specs=[
            pl.BlockSpec((T, D), lambda h, t: (t, 0)),
            pl.BlockSpec((HP, D, 4 * DK), lambda h, t: (h, 0, 0)),
            pl.BlockSpec((HP, 1, DK), lambda h, t: (h, 0, 0)),
            pl.BlockSpec((HP, 1, DK), lambda h, t: (h, 0, 0)),
        ],
        out_specs=pl.BlockSpec((T, HP * DK), lambda h, t: (t, h)),
        out_shape=jax.ShapeDtypeStruct((S, H * DK), BF16),
        scratch_shapes=[pltpu.VMEM((HP, DK, DK), F32)],
        compiler_params=_params("arbitrary", "arbitrary"),
        name="hgrn2",
    )(xb, w, lb.reshape(H, 1, DK), norm_g.reshape(H, 1, DK))


def _qkv_kernel(xb_ref, posr_ref, posc_ref, wqt_ref, wk_ref, wvt_ref, fcol_ref, frow_ref,
                qt_ref, k_ref, vt_ref, *, rows):
    H, HD, L = ATTN_HEADS, HEAD_DIM, MOBA_BLOCK
    x = xb_ref[...]
    qt = _dot_nt(wqt_ref[...], x)
    vt = _dot_nt(wvt_ref[...], x)
    k = _dot(x, wk_ref[...])

    ang_t = fcol_ref[...] * posr_ref[...].astype(F32)
    cos_t, sin_t = jnp.cos(ang_t), jnp.sin(ang_t)
    ang = posc_ref[...].astype(F32) * frow_ref[...]
    cos_r, sin_r = jnp.cos(ang), jnp.sin(ang)
    lane = lax.broadcasted_iota(jnp.int32, (rows, HD), 1)
    first = lane < ROPE_HALF
    sin_r = jnp.where(first, -sin_r, sin_r)

    for h in range(H):
        blk = qt[h * HD:(h + 1) * HD]
        x1, x2 = blk[:ROPE_HALF], blk[ROPE_HALF:ROPE_DIM]
        qh = jnp.concatenate([x1 * cos_t - x2 * sin_t, x2 * cos_t + x1 * sin_t, blk[ROPE_DIM:]], axis=0)
        qh = (qh * Q_SCALE).astype(BF16)
        kh = k[:, h * HD:(h + 1) * HD]
        partner = jnp.where(first, pltpu.roll(kh, HD - ROPE_HALF, 1), pltpu.roll(kh, ROPE_HALF, 1))
        kh = (kh * cos_r + partner * sin_r).astype(BF16)
        vh = vt[h * HD:(h + 1) * HD].astype(BF16)
        for n in range(rows // L):
            qt_ref[h, n] = qh[:, n * L:(n + 1) * L]
            vt_ref[h, n] = vh[:, n * L:(n + 1) * L]
            k_ref[h, n] = kh[n * L:(n + 1) * L]


def _attn_kernel(qt_ref, k_ref, vt_ref, o_ref, kmean_ref, sel_ref, s_ref, gmax_ref, m_ref, acc_ref, *, nblk):
    L = MOBA_BLOCK
    i = pl.program_id(1)

    G = ATTN_GROUP

    @pl.when(i == 0)
    def _():
        def mean_body(n, carry):
            kmean_ref[pl.ds(n, 1), :] = jnp.mean(k_ref[n].astype(F32), axis=0, keepdims=True)
            return carry
        lax.fori_loop(0, nblk, mean_body, 0)

    qt = qt_ref[...]
    km = kmean_ref[...]
    km_hi = km.astype(BF16)
    km_lo = (km - km_hi.astype(F32)).astype(BF16)
    gate = _dot(km_hi, qt) + _dot(km_lo, qt)

    kpos = lax.broadcasted_iota(jnp.int32, (L, L), 0)
    qpos = lax.broadcasted_iota(jnp.int32, (L, L), 1)
    s_own = jnp.where(kpos <= qpos, _dot(k_ref[i], qt), MASK_NEG)

    rowj = lax.broadcasted_iota(jnp.int32, (nblk, L), 0)
    gate = jnp.where(rowj < i, gate, -jnp.inf)
    sel = jnp.zeros((nblk, L), F32)
    for kk in range(MOBA_TOPK):
        mx = jnp.max(gate, axis=0, keepdims=True)
        idx = jnp.min(jnp.where(gate == mx, rowj, nblk), axis=0, keepdims=True)
        pick = jnp.logical_and(rowj == idx, kk < i)
        sel = jnp.where(pick, 1.0, sel)
        gate = jnp.where(pick, -jnp.inf, gate)
    sel_ref[:nblk] = sel

    def group_max(s):
        return jnp.max(s.reshape(L // SUBLANES, SUBLANES, L), axis=0)

    def with_ones(vt):
        return jnp.concatenate([vt, jnp.ones((BF16_SUBLANES, vt.shape[1]), BF16)], axis=0)

    m0 = jnp.max(s_own, axis=0, keepdims=True)
    m_ref[...] = m0
    acc_ref[...] = _dot(with_ones(vt_ref[i]), jnp.exp2(s_own - m0).astype(BF16))

    past = nblk - 1
    ngroups = pl.cdiv(past, G)
    half = G // 2

    def stage_scores(t, nb):
        buf, m8 = t % 2, None
        for u in range(min(nb, past - t * G)):
            jj = t * G + u
            s = jnp.where(sel_ref[jj:jj + 1, :] > 0.5, _dot(k_ref[jj], qt), MASK_NEG)
            s_ref[buf, u] = s
            m8 = group_max(s) if m8 is None else jnp.maximum(m8, group_max(s))
        gmax_ref[buf] = m8

    def stage_accumulate(t, nb):
        buf = t % 2
        m_old = m_ref[...]
        m_new = jnp.maximum(m_old, jnp.max(gmax_ref[buf], axis=0, keepdims=True))
        total = None
        for u in range(min(nb, past - t * G)):
            p = jnp.exp2(s_ref[buf, u] - m_new).astype(BF16)
            part = _dot(with_ones(vt_ref[t * G + u]), p)
            total = part if total is None else total + part
        acc_ref[...] = jnp.exp2(m_old - m_new) * acc_ref[...] + total
        m_ref[...] = m_new

    if ngroups > 0:
        stage_scores(0, G)
    for t in range(ngroups):
        real = i - t * G
        if t + 1 < ngroups:
            @pl.when(real - G > half)
            def _():
                stage_accumulate(t, G)
                stage_scores(t + 1, G)

            @pl.when(jnp.logical_and(real - G > 0, real - G <= half))
            def _():
                stage_accumulate(t, G)
                stage_scores(t + 1, half)

        @pl.when(jnp.logical_and(real > half, real <= G))
        def _():
            stage_accumulate(t, G)

        @pl.when(jnp.logical_and(real > 0, real <= half))
        def _():
            stage_accumulate(t, half)

    acc = acc_ref[...]
    o_ref[...] = (acc[:HEAD_DIM] / acc[HEAD_DIM:HEAD_DIM + 1]).T.astype(BF16)


def _moba(xb, positions, w_in):
    S, D = xb.shape
    H, HD, L = ATTN_HEADS, HEAD_DIM, MOBA_BLOCK
    assert S % L == 0
    NB = S // L
    T = min(ROW_TILE, S)
    nper = T // L
    wb = w_in.astype(BF16)
    wqt, wk, wvt = wb[:, :D].T, wb[:, D:2 * D], wb[:, 2 * D:].T
    inv_freq = 1.0 / (ROPE_THETA ** (jnp.arange(0, ROPE_DIM, 2, dtype=F32) / ROPE_DIM))
    fcol = inv_freq.reshape(ROPE_HALF, 1)
    frow = jnp.concatenate([inv_freq, inv_freq, jnp.zeros((HD - ROPE_DIM,), F32)]).reshape(1, HD)
    pos = positions.reshape(S).astype(jnp.int32)
    qt, k, vt = pl.pallas_call(
        functools.partial(_qkv_kernel, rows=T),
        grid=(S // T,),
        in_specs=[
            pl.BlockSpec((T, D), lambda i: (i, 0)),
            pl.BlockSpec((1, T), lambda i: (0, i)),
            pl.BlockSpec((T, 1), lambda i: (i, 0)),
            _const_spec((D, D)),
            _const_spec((D, D)),
            _const_spec((D, D)),
            _const_spec((ROPE_HALF, 1)),
            _const_spec((1, HD)),
        ],
        out_specs=[
            pl.BlockSpec((H, nper, HD, L), lambda i: (0, i, 0, 0)),
            pl.BlockSpec((H, nper, L, HD), lambda i: (0, i, 0, 0)),
            pl.BlockSpec((H, nper, HD, L), lambda i: (0, i, 0, 0)),
        ],
        out_shape=[
            jax.ShapeDtypeStruct((H, NB, HD, L), BF16),
            jax.ShapeDtypeStruct((H, NB, L, HD), BF16),
            jax.ShapeDtypeStruct((H, NB, HD, L), BF16),
        ],
        compiler_params=_params("arbitrary"),
        name="moba_qkv_rope",
    )(xb, pos.reshape(1, S), pos.reshape(S, 1), wqt, wk, wvt, fcol, frow)

    return pl.pallas_call(
        functools.partial(_attn_kernel, nblk=NB),
        grid=(H, NB),
        in_specs=[
            pl.BlockSpec((None, None, HD, L), lambda h, i: (h, i, 0, 0)),
            pl.BlockSpec((None, NB, L, HD), lambda h, i: (h, 0, 0, 0)),
            pl.BlockSpec((None, NB, HD, L), lambda h, i: (h, 0, 0, 0)),
        ],
        out_specs=pl.BlockSpec((L, HD), lambda h, i: (i, h)),
        out_shape=jax.ShapeDtypeStruct((S, H * HD), BF16),
        scratch_shapes=[
            pltpu.VMEM((NB, HD), F32),
            pltpu.VMEM((NB, L), F32),
            pltpu.VMEM((2, ATTN_GROUP, L, L), F32),
            pltpu.VMEM((2, SUBLANES, L), F32),
            pltpu.VMEM((1, L), F32),
            pltpu.VMEM((HD + BF16_SUBLANES, L), F32),
        ],
        compiler_params=_params("arbitrary", "arbitrary"),
        name="moba_attention",
    )(qt, k, vt)


def _hgrn_lower_bound(lower_bounds, layer):
    s = jax.nn.softmax(lower_bounds.astype(F32), axis=0)
    return (jnp.cumsum(s, axis=0) - s[0])[layer]


def kernel(x, positions, hgrn_lower_bounds, l0_mix_w_in, l0_mix_norm_g, l0_mix_w_out, l0_ln1_g, l0_ln1_b, l0_ffn_w_up, l0_ffn_conv, l0_ffn_w_down, l0_ln2_g, l0_ln2_b, l1_mix_w_in, l1_mix_w_out, l1_ln1_g, l1_ln1_b, l1_ffn_w_up, l1_ffn_conv, l1_ffn_w_down, l1_ln2_g, l1_ln2_b, l2_mix_w_in, l2_mix_conv, l2_mix_w_out, l2_ln1_g, l2_ln1_b, l2_ffn_w_up, l2_ffn_conv, l2_ffn_w_down, l2_ln2_g, l2_ln2_b, l3_mix_w_in, l3_mix_norm_g, l3_mix_w_out, l3_ln1_g, l3_ln1_b, l3_ffn_w_up, l3_ffn_conv, l3_ffn_w_down, l3_ln2_g, l3_ln2_b):
    layers = [
        ((l0_mix_w_in, l0_mix_norm_g, l0_mix_w_out), l0_ln1_g, l0_ln1_b,
         (l0_ffn_w_up, l0_ffn_conv, l0_ffn_w_down), l0_ln2_g, l0_ln2_b),
        ((l1_mix_w_in, l1_mix_w_out), l1_ln1_g, l1_ln1_b,
         (l1_ffn_w_up, l1_ffn_conv, l1_ffn_w_down), l1_ln2_g, l1_ln2_b),
        ((l2_mix_w_in, l2_mix_conv, l2_mix_w_out), l2_ln1_g, l2_ln1_b,
         (l2_ffn_w_up, l2_ffn_conv, l2_ffn_w_down), l2_ln2_g, l2_ln2_b),
        ((l3_mix_w_in, l3_mix_norm_g, l3_mix_w_out), l3_ln1_g, l3_ln1_b,
         (l3_ffn_w_up, l3_ffn_conv, l3_ffn_w_down), l3_ln2_g, l3_ln2_b),
    ]
    B, S, D = x.shape
    outs = []
    for bi in range(B):
        xf = x[bi]
        xb = xf.astype(BF16)
        for i in range(DEPTH):
            mix_p, ln1_g, ln1_b, ffn_p, ln2_g, ln2_b = layers[i]
            kind = i % N_MIXERS
            if kind == 0:
                w_in, norm_g, w_out = mix_p
                o = _hgrn(xb, _hgrn_lower_bound(hgrn_lower_bounds, i), w_in, norm_g)
                xf, xb = _ffn(xf, *ffn_p, ln2_g, ln2_b, mixer=(o, w_out, ln1_g, ln1_b))
            elif kind == 1:
                w_in, w_out = mix_p
                o = _moba(xb, positions[bi], w_in)
                xf, xb = _ffn(xf, *ffn_p, ln2_g, ln2_b, mixer=(o, w_out, ln1_g, ln1_b))
            else:
                w_in, conv_w, w_out = mix_p
                xf, xb = _ffn(xf, *ffn_p, ln2_g, ln2_b, mixer=(xb, w_in, conv_w, w_out, ln1_g, ln1_b))
        outs.append(xf)
    return outs[0][None] if B == 1 else jnp.stack(outs, axis=0)
```
